```python
import math
import jax, jax.numpy as jnp
from jax import lax
import numpy as np

D_MODEL = 1024
BATCH = 2
SEQ = 8192
DEPTH = 1
DEC_BATCH = 128
DEC_SEQ = 4
PAST_LEN = 2048
PAGE_SIZE = 128

SB_HEADS = 8
SB_HEAD_DIM = D_MODEL // 16
SB_DIM = SB_HEADS * SB_HEAD_DIM
SB_Q_BLOCK = 128
SB_BIAS_LO = -7.0
SB_BIAS_HI = -5.0
GDN_HEADS = 4
GDN_HEAD_DIM = D_MODEL // 8
GDN_DIM = GDN_HEADS * GDN_HEAD_DIM
CONV_WIDTH = 4
CONV_DIM = 3 * GDN_DIM
GDN_CHUNK = 64
D_FF = 4 * D_MODEL
NORM_EPS = 1e-6
L2_EPS = 1e-6
IN_SPLITS = (SB_DIM, SB_DIM, SB_DIM, CONV_DIM, GDN_DIM, GDN_HEADS, GDN_HEADS, D_MODEL, D_MODEL)
IN_DIM = sum(IN_SPLITS)

kernel_name = "stickbreak_gdn_parallel_hybrid_step"

F32 = jnp.float32


def rmsnorm(x, w):
    xf = x.astype(F32)
    return xf * lax.rsqrt(jnp.mean(xf * xf, axis=-1, keepdims=True) + NORM_EPS) * w.astype(F32)


def l2norm(x):
    return x * lax.rsqrt(jnp.sum(x * x, axis=-1, keepdims=True) + L2_EPS)


def stick_breaking(q, k, v, q_pos, k_pos, sb_bias):
    z = (jnp.einsum('bqhd,bkhd->bhqk', q, k) * (q.shape[-1] ** -0.5)
         + sb_bias.astype(F32)[None, :, None, None])
    mask = k_pos[None, :] < q_pos[:, None]
    log_1m = jnp.where(mask, jax.nn.log_sigmoid(-z), 0.0)
    suffix = lax.cumsum(log_1m, axis=3, reverse=True) - log_1m
    w = jnp.where(mask, jnp.exp(jax.nn.log_sigmoid(z) + suffix), 0.0)
    return jnp.einsum('bhqk,bkhd->bqhd', w, v)


def sb_prompt(q, k, v, sb_bias):
    bsz, t_len, n_h, d = q.shape
    n_blk = t_len // SB_Q_BLOCK
    qb = jnp.moveaxis(q.reshape(bsz, n_blk, SB_Q_BLOCK, n_h, d), 1, 0)
    pb = jnp.arange(t_len).reshape(n_blk, SB_Q_BLOCK)
    k_pos = jnp.arange(t_len)
    out = lax.map(lambda a: stick_breaking(a[0], k, v, a[1], k_pos, sb_bias), (qb, pb))
    return jnp.moveaxis(out, 0, 1).reshape(bsz, t_len, n_h, d)


def sb_extend(q, k, v, k_past, v_past, sb_bias):
    past = k_past.shape[1]
    t_len = q.shape[1]
    k_all = jnp.concatenate([k_past, k], axis=1)
    v_all = jnp.concatenate([v_past, v], axis=1)
    q_pos = past + jnp.arange(t_len)
    k_pos = jnp.arange(past + t_len)
    return stick_breaking(q, k_all, v_all, q_pos, k_pos, sb_bias)


def short_conv(u_ext, w_conv):
    out = lax.conv_general_dilated(u_ext, w_conv[:, None, :], window_strides=(1,), padding='VALID',
                                   dimension_numbers=('NWC', 'WIO', 'NWC'),
                                   feature_group_count=u_ext.shape[-1])
    return jax.nn.silu(out)


def gated_delta_chunked(q, k, v, g, beta, s0, chunk):
    bsz, t_len, n_h, _ = q.shape
    d_v = v.shape[-1]
    n_chunks = t_len // chunk

    def to_chunks(a):
        a = a.reshape((bsz, n_chunks, chunk) + a.shape[2:])
        return jnp.moveaxis(jnp.swapaxes(a, 2, 3), 1, 0)

    incl = jnp.tril(jnp.ones((chunk, chunk), dtype=bool))
    strict = jnp.tril(jnp.ones((chunk, chunk), dtype=bool), -1)
    eye = jnp.eye(chunk, dtype=F32)

    def step(s, inp):
        qc, kc, vc, gc, bc = inp
        cg = jnp.cumsum(gc, axis=-1)
        decay = jnp.exp(jnp.where(incl, cg[..., :, None] - cg[..., None, :], -jnp.inf))
        m = jnp.where(strict, bc[..., :, None] * jnp.einsum('bhik,bhjk->bhij', kc, kc) * decay, 0.0)
        rhs = jnp.concatenate([vc * bc[..., None], kc * (bc * jnp.exp(cg))[..., None]], axis=-1)
        sol = lax.linalg.triangular_solve(eye + m, rhs, left_side=True, lower=True, unit_diagonal=True)
        u, w = sol[..., :d_v], sol[..., d_v:]
        v_new = u - jnp.einsum('bhck,bhkv->bhcv', w, s)
        o = (jnp.einsum('bhck,bhkv->bhcv', qc * jnp.exp(cg)[..., None], s)
             + jnp.einsum('bhij,bhjv->bhiv', jnp.einsum('bhik,bhjk->bhij', qc, kc) * decay, v_new))
        g_last = cg[..., -1:]
        s = s * jnp.exp(g_last)[..., None] + jnp.einsum('bhck,bhcv->bhkv', kc * jnp.exp(g_last - cg)[..., None], v_new)
        return s, o

    s_fin, o = lax.scan(step, s0, (to_chunks(q), to_chunks(k), to_chunks(v), to_chunks(g), to_chunks(beta)))
    o = jnp.swapaxes(jnp.moveaxis(o, 0, 1), 2, 3).reshape(bsz, t_len, n_h, d_v)
    return o, s_fin


def gdn_branch(conv_out, z, a, b, s0, a_log, dt_bias, gdn_norm_w):
    bsz, t_len, _ = conv_out.shape
    q, k, v = jnp.split(conv_out, 3, axis=-1)
    shp = (bsz, t_len, GDN_HEADS, GDN_HEAD_DIM)
    q = l2norm(q.reshape(shp)) * (GDN_HEAD_DIM ** -0.5)
    k = l2norm(k.reshape(shp))
    v = v.reshape(shp)
    beta = jax.nn.sigmoid(b)
    g = -jnp.exp(a_log.astype(F32)) * jax.nn.softplus(a + dt_bias.astype(F32))
    chunk = math.gcd(GDN_CHUNK, t_len)
    o, s_new = gated_delta_chunked(q, k, v, g, beta, s0.astype(F32), chunk)
    o = o * lax.rsqrt(jnp.mean(o * o, axis=-1, keepdims=True) + NORM_EPS) * gdn_norm_w.astype(F32)
    o = o * jax.nn.silu(z.reshape(shp))
    return o.reshape(bsz, t_len, GDN_DIM), s_new


def decoder_layer(x, k_past, v_past, conv_prev, ssm_prev, norm_mix_w, w_in, sb_bias, conv_w, a_log, dt_bias,
                  gdn_norm_w, w_pa, w_pb, w_o, norm_mlp_w, w_up, w_down):
    bsz, t_len, _ = x.shape
    h = rmsnorm(x, norm_mix_w)
    p = h @ w_in.astype(F32)
    q_a, k_a, v_a, u, z, a, b, g_a, g_b = jnp.split(p, [int(i) for i in np.cumsum(IN_SPLITS)[:-1]], axis=-1)
    hs = (bsz, t_len, SB_HEADS, SB_HEAD_DIM)
    q_a, k_a, v_a = q_a.reshape(hs), k_a.reshape(hs), v_a.reshape(hs)
    if k_past is None:
        o_a = sb_prompt(q_a, k_a, v_a, sb_bias)
    else:
        o_a = sb_extend(q_a, k_a, v_a, k_past.astype(F32), v_past.astype(F32), sb_bias)
    u_ext = jnp.concatenate([conv_prev.astype(F32), u], axis=1)
    conv_new = u_ext[:, -(CONV_WIDTH - 1):]
    o_b, ssm_new = gdn_branch(short_conv(u_ext, conv_w.astype(F32)), z, a, b, ssm_prev, a_log, dt_bias, gdn_norm_w)
    y_a = o_a.reshape(bsz, t_len, SB_DIM) @ w_pa.astype(F32)
    y_b = o_b @ w_pb.astype(F32)
    mix = (jax.nn.sigmoid(g_a) * y_a + jax.nn.sigmoid(g_b) * y_b) @ w_o.astype(F32)
    x = x + mix
    hm = rmsnorm(x, norm_mlp_w)
    x = x + jnp.square(jax.nn.relu(hm @ w_up.astype(F32))) @ w_down.astype(F32)
    return x, k_a, v_a, conv_new, ssm_new


def setup_inputs(seed: int = 0) -> dict:
    key = jax.random.key(seed)
    ks = jax.random.split(key, 24)
    n_pages = PAST_LEN // PAGE_SIZE
    n_used = DEC_BATCH * n_pages
    n_pool = n_used + n_used // 4
    nrm = jax.random.normal
    x_prompt = nrm(ks[0], (BATCH, SEQ, D_MODEL), F32)
    x_sample = nrm(ks[1], (DEC_BATCH, DEC_SEQ, D_MODEL), F32)
    cache_k = nrm(ks[2], (DEPTH, n_pool, PAGE_SIZE, SB_HEADS, SB_HEAD_DIM), F32)
    cache_v = nrm(ks[3], (DEPTH, n_pool, PAGE_SIZE, SB_HEADS, SB_HEAD_DIM), F32)
    page_table = jax.random.permutation(ks[4], n_pool)[:n_used].reshape(DEC_BATCH, n_pages).astype(jnp.int32)
    state_conv = nrm(ks[5], (DEPTH, DEC_BATCH, CONV_WIDTH - 1, CONV_DIM), F32)
    state_ssm = 0.1 * nrm(ks[6], (DEPTH, DEC_BATCH, GDN_HEADS, GDN_HEAD_DIM, GDN_HEAD_DIM), F32)
    norm_mix_w = 1.0 + 0.02 * nrm(ks[7], (DEPTH, D_MODEL), F32)
    w_in = nrm(ks[8], (DEPTH, D_MODEL, IN_DIM), F32) * D_MODEL ** -0.5
    sb_bias = jax.random.uniform(ks[20], (DEPTH, SB_HEADS), F32, SB_BIAS_LO, SB_BIAS_HI)
    conv_w = nrm(ks[9], (DEPTH, CONV_WIDTH, CONV_DIM), F32) * CONV_WIDTH ** -0.5
    a_log = jnp.log(jax.random.uniform(ks[10], (DEPTH, GDN_HEADS), F32, 1.0, 16.0))
    dt = jnp.exp(jax.random.uniform(ks[11], (DEPTH, GDN_HEADS), F32, math.log(1e-3), math.log(1e-1)))
    dt_bias = dt + jnp.log(-jnp.expm1(-dt))
    gdn_norm_w = 1.0 + 0.02 * nrm(ks[12], (DEPTH, GDN_HEAD_DIM), F32)
    w_pa = nrm(ks[13], (DEPTH, SB_DIM, D_MODEL), F32) * SB_DIM ** -0.5
    w_pb = nrm(ks[14], (DEPTH, GDN_DIM, D_MODEL), F32) * GDN_DIM ** -0.5
    w_o = nrm(ks[15], (DEPTH, D_MODEL, D_MODEL), F32) * D_MODEL ** -0.5
    norm_mlp_w = 1.0 + 0.02 * nrm(ks[16], (DEPTH, D_MODEL), F32)
    w_up = nrm(ks[17], (DEPTH, D_MODEL, D_FF), F32) * D_MODEL ** -0.5
    w_down = nrm(ks[18], (DEPTH, D_FF, D_MODEL), F32) * D_FF ** -0.5
    norm_final_w = 1.0 + 0.02 * nrm(ks[19], (D_MODEL,), F32)
    return {"x_prompt": x_prompt, "x_sample": x_sample, "cache_k": cache_k, "cache_v": cache_v,
            "page_table": page_table, "state_conv": state_conv, "state_ssm": state_ssm,
            "norm_mix_w": norm_mix_w, "w_in": w_in, "sb_bias": sb_bias, "conv_w": conv_w, "a_log": a_log,
            "dt_bias": dt_bias, "gdn_norm_w": gdn_norm_w, "w_pa": w_pa, "w_pb": w_pb, "w_o": w_o,
            "norm_mlp_w": norm_mlp_w, "w_up": w_up, "w_down": w_down, "norm_final_w": norm_final_w}


def reference(x_prompt, x_sample, cache_k, cache_v, page_table, state_conv, state_ssm,
              norm_mix_w, w_in, sb_bias, conv_w, a_log, dt_bias, gdn_norm_w, w_pa, w_pb, w_o,
              norm_mlp_w, w_up, w_down, norm_final_w):
    b_p = x_prompt.shape[0]
    b_s = x_sample.shape[0]
    h_p = x_prompt.astype(F32)
    h_s = x_sample.astype(F32)
    conv0 = jnp.zeros((b_p, CONV_WIDTH - 1, CONV_DIM), F32)
    ssm0 = jnp.zeros((b_p, GDN_HEADS, GDN_HEAD_DIM, GDN_HEAD_DIM), F32)
    kp_l, vp_l, ks_l, vs_l, cp_l, cs_l, sp_l, ss_l = [], [], [], [], [], [], [], []
    for l in range(DEPTH):
        lw = (norm_mix_w[l], w_in[l], sb_bias[l], conv_w[l], a_log[l], dt_bias[l], gdn_norm_w[l], w_pa[l],
              w_pb[l], w_o[l], norm_mlp_w[l], w_up[l], w_down[l])
        h_p, k_p, v_p, c_p, s_p = decoder_layer(h_p, None, None, conv0, ssm0, *lw)
        k_past = cache_k[l][page_table].reshape(b_s, -1, SB_HEADS, SB_HEAD_DIM)
        v_past = cache_v[l][page_table].reshape(b_s, -1, SB_HEADS, SB_HEAD_DIM)
        h_s, k_s, v_s, c_s, s_s = decoder_layer(h_s, k_past, v_past, state_conv[l], state_ssm[l], *lw)
        kp_l.append(k_p); vp_l.append(v_p); ks_l.append(k_s); vs_l.append(v_s)
        cp_l.append(c_p); cs_l.append(c_s); sp_l.append(s_p); ss_l.append(s_s)
    y_prompt = rmsnorm(h_p, norm_final_w).astype(x_prompt.dtype)
    y_sample = rmsnorm(h_s, norm_final_w).astype(x_sample.dtype)
    page_shape = (DEPTH, b_p, -1, PAGE_SIZE, SB_HEADS, SB_HEAD_DIM)
    new_k_prompt = jnp.stack(kp_l).reshape(page_shape).astype(cache_k.dtype)
    new_v_prompt = jnp.stack(vp_l).reshape(page_shape).astype(cache_v.dtype)
    new_k_sample = jnp.stack(ks_l).astype(cache_k.dtype)
    new_v_sample = jnp.stack(vs_l).astype(cache_v.dtype)
    new_conv_prompt = jnp.stack(cp_l).astype(state_conv.dtype)
    new_conv_sample = jnp.stack(cs_l).astype(state_conv.dtype)
    new_ssm_prompt = jnp.stack(sp_l).astype(state_ssm.dtype)
    new_ssm_sample = jnp.stack(ss_l).astype(state_ssm.dtype)
    return (y_prompt, y_sample, new_k_prompt, new_v_prompt, new_k_sample, new_v_sample,
            new_conv_prompt, new_conv_sample, new_ssm_prompt, new_ssm_sample)
```

```python
import functools
import math

import jax
import jax.numpy as jnp
from jax import lax
from jax.experimental import pallas as pl
from jax.experimental.pallas import tpu as pltpu

F32 = jnp.float32
BF16 = jnp.bfloat16

D_MODEL = 1024
SB_HEADS = 8
SB_HEAD_DIM = 64
SB_DIM = SB_HEADS * SB_HEAD_DIM
GDN_HEADS = 4
GDN_HEAD_DIM = 128
GDN_DIM = GDN_HEADS * GDN_HEAD_DIM
CONV_WIDTH = 4
CONV_DIM = 3 * GDN_DIM
GDN_CHUNK = 64
D_FF = 4 * D_MODEL
NORM_EPS = 1e-6
L2_EPS = 1e-6

LANES = 128
SUBLANES = 8
VMEM_LIMIT_BYTES = 56 * 1024 * 1024

COL_Q = 0
COL_K = COL_Q + SB_DIM
COL_V = COL_K + SB_DIM
COL_U = COL_V + SB_DIM
COL_Z = COL_U + CONV_DIM
COL_GA = COL_Z + GDN_DIM
COL_GB = COL_GA + D_MODEL
COL_AB = COL_GB + D_MODEL
IN_COLS = COL_AB + LANES

SB_BLOCK = 256
PREP_ROWS = 512
DEC_GROUP = 8


def _dot(a, b):
    return jnp.dot(a, b, preferred_element_type=F32)


def _dot_nt(a, b):
    return lax.dot_general(a, b, (((1,), (1,)), ((), ())), preferred_element_type=F32)


def _dot_tn(a, b):
    return lax.dot_general(a, b, (((0,), (0,)), ((), ())), preferred_element_type=F32)


def _hi_lo(x):
    hi = x.astype(BF16)
    lo = (x - hi.astype(F32)).astype(BF16)
    return hi, lo


def _dot3(a, b):
    ah, al = _hi_lo(a)
    bh, bl = _hi_lo(b)
    return _dot(ah, bh) + (_dot(ah, bl) + _dot(al, bh))


def _dot_exact01(m01, x):
    h = x.astype(BF16)
    r = x - h.astype(F32)
    m = r.astype(BF16)
    l = (r - m.astype(F32)).astype(BF16)
    return _dot(m01, h) + (_dot(m01, m) + _dot(m01, l))


def _softplus(z):
    return jnp.maximum(z, 0.0) + jnp.log(1.0 + jnp.exp(-jnp.abs(z)))


def _sigmoid(z):
    return 1.0 / (1.0 + jnp.exp(-z))


def _silu(z):
    return z * _sigmoid(z)


def _rmsnorm(x, w):
    return x * lax.rsqrt(jnp.mean(x * x, axis=-1, keepdims=True) + NORM_EPS) * w


def _const_spec(shape):
    nd = len(shape)
    return pl.BlockSpec(shape, lambda *_: (0,) * nd, pipeline_mode=pl.Buffered(1))


def _params(*sem):
    return pltpu.CompilerParams(dimension_semantics=sem, vmem_limit_bytes=VMEM_LIMIT_BYTES)


def _inproj_kernel(x_ref, nw_ref, w_ref, q_ref, k_ref, v_ref, kb_ref, vb_ref, u_ref, z_ref, ab_ref, sg_ref):
    x = x_ref[...]
    hb = _rmsnorm(x, nw_ref[...]).astype(BF16)

    def seg(lo, width):
        return _dot(hb, w_ref[:, lo:lo + width])

    q_ref[...] = (seg(COL_Q, SB_DIM) * (SB_HEAD_DIM ** -0.5)).astype(BF16)
    k = seg(COL_K, SB_DIM)
    k_ref[...] = k
    kb_ref[...] = k.astype(BF16)
    v = seg(COL_V, SB_DIM)
    v_ref[...] = v
    vb_ref[...] = v.astype(BF16)
    for j in range(CONV_DIM // SB_DIM):
        u_ref[:, j * SB_DIM:(j + 1) * SB_DIM] = seg(COL_U + j * SB_DIM, SB_DIM)
    z_ref[...] = seg(COL_Z, GDN_DIM)
    for j in range(2 * D_MODEL // SB_DIM):
        sg_ref[:, j * SB_DIM:(j + 1) * SB_DIM] = _sigmoid(seg(COL_GA + j * SB_DIM, SB_DIM))
    ab_ref[...] = seg(COL_AB, LANES)


def _inproj(x2d, norm_w, w_perm, tm):
    n = x2d.shape[0]
    row = lambda width: pl.BlockSpec((tm, width), lambda i: (i, 0))
    out_shapes = (
        jax.ShapeDtypeStruct((n, SB_DIM), BF16),
        jax.ShapeDtypeStruct((n, SB_DIM), F32),
        jax.ShapeDtypeStruct((n, SB_DIM), F32),
        jax.ShapeDtypeStruct((n, SB_DIM), BF16),
        jax.ShapeDtypeStruct((n, SB_DIM), BF16),
        jax.ShapeDtypeStruct((n, CONV_DIM), F32),
        jax.ShapeDtypeStruct((n, GDN_DIM), F32),
        jax.ShapeDtypeStruct((n, LANES), F32),
        jax.ShapeDtypeStruct((n, 2 * D_MODEL), F32),
    )
    return pl.pallas_call(
        _inproj_kernel,
        grid=(n // tm,),
        in_specs=[row(D_MODEL), _const_spec((1, D_MODEL)), _const_spec((D_MODEL, IN_COLS))],
        out_specs=tuple(row(s.shape[1]) for s in out_shapes),
        out_shape=out_shapes,
        compiler_params=_params("parallel"),
        name="inproj",
    )(x2d, norm_w, w_perm)


def _sb_weights(z, carry, tri, mask):
    lm = -_softplus(z)
    if mask is not None:
        lm = jnp.where(mask, lm, 0.0)
    hi, lo = _hi_lo(lm)
    suffix = _dot(hi, tri) + _dot(lo, tri)
    w = jnp.exp(z + lm + suffix + carry)
    if mask is not None:
        w = jnp.where(mask, w, 0.0)
    return w.astype(BF16), carry + jnp.sum(lm, axis=1, keepdims=True)


def _sb_prompt_kernel(bias_ref, q_ref, k_ref, v_ref, tri_ref, o_ref):
    hp = pl.program_id(1)
    qi = pl.program_id(2)
    tb = SB_BLOCK
    lane = lax.broadcasted_iota(jnp.int32, (1, LANES), 1)
    head_lanes = [lane < SB_HEAD_DIM, lane >= SB_HEAD_DIM]
    q = q_ref[0]
    zero = jnp.zeros((), BF16)
    qm = [jnp.where(head_lanes[j], q, zero) for j in range(2)]
    bias = [bias_ref[2 * hp + j] for j in range(2)]
    tri = tri_ref[...]
    row = lax.broadcasted_iota(jnp.int32, (tb, tb), 0)
    col = lax.broadcasted_iota(jnp.int32, (tb, tb), 1)
    causal = col < row

    def block(kb, acc, carries, mask):
        start = pl.multiple_of(kb * tb, tb)
        k = k_ref[0, pl.ds(start, tb), :]
        v = v_ref[0, pl.ds(start, tb), :]
        new_carries = []
        for j in range(2):
            z = _dot_nt(qm[j], k) + bias[j]
            vm = jnp.where(head_lanes[j], v, zero)
            w, c = _sb_weights(z, carries[j], tri, mask)
            acc = acc + _dot(w, vm)
            new_carries.append(c)
        return acc, tuple(new_carries)

    acc0 = jnp.zeros((tb, LANES), F32)
    c0 = (jnp.zeros((tb, 1), F32), jnp.zeros((tb, 1), F32))
    acc, carries = block(qi, acc0, c0, causal)

    def body(i, state):
        acc, carries = state
        return block(qi - 1 - i, acc, carries, None)

    acc, _ = lax.fori_loop(0, qi, body, (acc, carries))
    o_ref[0] = acc.astype(BF16)


def _sb_prompt(sb_bias, q, k, v, tri):
    bsz, t_len, _ = q.shape
    tb = SB_BLOCK
    return pl.pallas_call(
        _sb_prompt_kernel,
        grid_spec=pltpu.PrefetchScalarGridSpec(
            num_scalar_prefetch=1,
            grid=(bsz, SB_DIM // LANES, t_len // tb),
            in_specs=[
                pl.BlockSpec((1, tb, LANES), lambda b, h, i, *_: (b, i, h)),
                pl.BlockSpec((1, t_len, LANES), lambda b, h, i, *_: (b, 0, h)),
                pl.BlockSpec((1, t_len, LANES), lambda b, h, i, *_: (b, 0, h)),
                pl.BlockSpec((tb, tb), lambda b, h, i, *_: (0, 0)),
            ],
            out_specs=pl.BlockSpec((1, tb, LANES), lambda b, h, i, *_: (b, i, h)),
        ),
        out_shape=jax.ShapeDtypeStruct((bsz, t_len, SB_DIM), BF16),
        compiler_params=_params("parallel", "parallel", "arbitrary"),
        name="sb_prompt",
    )(sb_bias, q, k, v, tri)


def _sb_decode_kernel(n_pages, t_new, pt_ref, bias_ref, q_ref, kn_ref, vn_ref, tri_ref, *refs):
    kt_pages = refs[:n_pages]
    vt_pages = refs[n_pages:2 * n_pages]
    o_ref = refs[2 * n_pages]
    page = kt_pages[0].shape[2]
    rows = t_new * SB_HEADS
    r_iota = lax.broadcasted_iota(jnp.int32, (rows, SB_DIM), 0)
    l_iota = lax.broadcasted_iota(jnp.int32, (rows, SB_DIM), 1)
    head_of_row = r_iota % SB_HEADS
    own_lanes = (l_iota // SB_HEAD_DIM) == head_of_row

    q = q_ref[0].astype(F32)
    q_rep = jnp.broadcast_to(q[:, None, :], (t_new, SB_HEADS, SB_DIM)).reshape(rows, SB_DIM)
    q_bd = jnp.where(own_lanes, q_rep, 0.0).astype(BF16)

    r1 = lax.broadcasted_iota(jnp.int32, (rows, 1), 0)
    bias = jnp.zeros((rows, 1), F32)
    for h in range(SB_HEADS):
        bias = jnp.where(r1 % SB_HEADS == h, bias_ref[h], bias)

    pad = jnp.zeros((page - SUBLANES, SB_DIM), F32)
    k_new = jnp.concatenate([kn_ref[0], pad], axis=0).astype(BF16)
    v_new = jnp.concatenate([vn_ref[0], pad], axis=0).astype(BF16)
    kcol = lax.broadcasted_iota(jnp.int32, (rows, page), 1)
    qtok = lax.broadcasted_iota(jnp.int32, (rows, page), 0) // SB_HEADS
    z = _dot_nt(q_bd, k_new) + bias
    w, carry = _sb_weights(z, jnp.zeros((rows, 1), F32), tri_ref[:page, :page], kcol < qtok)
    acc = _dot(w, v_new)

    blk = tri_ref.shape[0] // page
    for pb in range(n_pages // blk - 1, -1, -1):
        kt = jnp.concatenate([kt_pages[pb * blk + i][0] for i in range(blk)], axis=1).astype(BF16)
        vt = jnp.concatenate([vt_pages[pb * blk + i][0] for i in range(blk)], axis=1).astype(BF16)
        z = _dot(q_bd, kt) + bias
        w, carry = _sb_weights(z, carry, tri_ref[...], None)
        acc = acc + _dot_nt(w, vt)

    acc = jnp.where(own_lanes, acc, 0.0)
    o_ref[0] = jnp.sum(acc.reshape(t_new, SB_HEADS, SB_DIM), axis=1).astype(BF16)


def _sb_decode(page_table, sb_bias, q, k_new8, v_new8, tri, cache_kt, cache_vt):
    n_seq, t_new, _ = q.shape
    n_pages = page_table.shape[1]
    page = cache_kt.shape[2]

    def page_spec(p):
        return pl.BlockSpec((1, SB_DIM, page), lambda s, pt, b: (pt[s, p], 0, 0))

    seq_spec = lambda r: pl.BlockSpec((1, r, SB_DIM), lambda s, pt, b: (s, 0, 0))
    return pl.pallas_call(
        functools.partial(_sb_decode_kernel, n_pages, t_new),
        grid_spec=pltpu.PrefetchScalarGridSpec(
            num_scalar_prefetch=2,
            grid=(n_seq,),
            in_specs=[seq_spec(t_new), seq_spec(SUBLANES), seq_spec(SUBLANES),
                      pl.BlockSpec(tri.shape, lambda s, pt, b: (0, 0))]
                     + [page_spec(p) for p in range(n_pages)] * 2,
            out_specs=seq_spec(t_new),
        ),
        out_shape=jax.ShapeDtypeStruct((n_seq, t_new, SB_DIM), BF16),
        compiler_params=_params("parallel"),
        name="sb_decode",
    )(page_table, sb_bias, q, k_new8, v_new8, tri, *([cache_kt] * n_pages), *([cache_vt] * n_pages))


def _conv_silu(ext_ref, first, rows, cw_ref):
    acc = None
    for w in range(CONV_WIDTH):
        term = ext_ref[first + w:first + w + rows, :] * cw_ref[w:w + 1, :]
        acc = term if acc is None else acc + term
    return _silu(acc)


def _l2norm(x):
    return x * lax.rsqrt(jnp.sum(x * x, axis=-1, keepdims=True) + L2_EPS)


def _gate_terms(ab, alog_ref, dtb_ref):
    g = -jnp.exp(alog_ref[...]) * _softplus(ab + dtb_ref[...])
    return g, _sigmoid(ab)


def _gated_out_norm(o, z, gw):
    o = o * lax.rsqrt(jnp.mean(o * o, axis=-1, keepdims=True) + NORM_EPS) * gw
    return o * _silu(z)


def _gdn_prep_kernel(u_ref, prev_ref, ab_ref, cw_ref, alog_ref, dtb_ref, lincl_ref,
                     un_ref, wn_ref, qg_ref, kg_ref, att_ref, cg_ref, ext_ref):
    rows = u_ref.shape[1]
    c_len = GDN_CHUNK
    dk = GDN_HEAD_DIM
    first_block = pl.program_id(1) == 0
    prev = prev_ref[0]
    ext_ref[0:SUBLANES, :] = jnp.where(first_block, jnp.zeros_like(prev), prev)
    ext_ref[SUBLANES:, :] = u_ref[0]
    conv = _conv_silu(ext_ref, SUBLANES - (CONV_WIDTH - 1), rows, cw_ref)

    g_all, beta_all = _gate_terms(ab_ref[0], alog_ref, dtb_ref)
    r_i = lax.broadcasted_iota(jnp.int32, (c_len, c_len), 0)
    c_i = lax.broadcasted_iota(jnp.int32, (c_len, c_len), 1)
    incl = c_i <= r_i
    strict = c_i < r_i
    eye = (c_i == r_i).astype(F32)
    lincl = lincl_ref[...]

    for pair in range(rows // (2 * c_len)):
        p0 = pair * 2 * c_len
        cg_pair = jnp.concatenate(
            [_dot_exact01(lincl, g_all[p0 + i * c_len:p0 + (i + 1) * c_len, :]) for i in range(2)], axis=0)
        cg_ref[0, p0:p0 + 2 * c_len, :] = cg_pair
        cg_t = cg_pair.T
        for i in range(2):
            r0 = p0 + i * c_len
            cg = cg_pair[i * c_len:(i + 1) * c_len, :]
            for h in range(GDN_HEADS):
                qc = _l2norm(conv[r0:r0 + c_len, h * dk:(h + 1) * dk]) * (dk ** -0.5)
                kc = _l2norm(conv[r0:r0 + c_len, GDN_DIM + h * dk:GDN_DIM + (h + 1) * dk])
                vc = conv[r0:r0 + c_len, 2 * GDN_DIM + h * dk:2 * GDN_DIM + (h + 1) * dk]
                cg_col = cg[:, h:h + 1]
                cg_row = cg_t[h:h + 1, i * c_len:(i + 1) * c_len]
                bc = beta_all[r0:r0 + c_len, GDN_HEADS + h:GDN_HEADS + h + 1]
                decay = jnp.exp(jnp.where(incl, cg_col - cg_row, -jnp.inf))
                kc_hi, kc_lo = _hi_lo(kc)
                kk = _dot_nt(kc_hi, kc_hi) + (_dot_nt(kc_hi, kc_lo) + _dot_nt(kc_lo, kc_hi))
                neg_m = jnp.where(strict, -(bc * kk * decay), 0.0)
                inv = eye + neg_m
                power = neg_m
                for _ in range(int(math.log2(c_len)) - 1):
                    power = _dot3(power, power)
                    inv = inv + _dot3(inv, power)
                rhs = jnp.concatenate([vc * bc, kc * (bc * jnp.exp(cg_col))], axis=1)
                sol = _dot3(inv, rhs)
                un_ref[0, r0:r0 + c_len, h * dk:(h + 1) * dk] = sol[:, :dk]
                wn_ref[0, r0:r0 + c_len, h * dk:(h + 1) * dk] = sol[:, dk:].astype(BF16)
                qg_ref[0, r0:r0 + c_len, h * dk:(h + 1) * dk] = (qc * jnp.exp(cg_col)).astype(BF16)
                g_last = cg[c_len - 1:c_len, h:h + 1]
                kg_ref[0, r0:r0 + c_len, h * dk:(h + 1) * dk] = (kc * jnp.exp(g_last - cg_col)).astype(BF16)
                qk = _dot_nt(qc.astype(BF16), kc.astype(BF16))
                att_ref[0, h, r0:r0 + c_len, :] = (qk * decay).astype(BF16)


def _gdn_prep(u, ab, conv_w, alog_pad, dtb_pad, lincl):
    bsz, t_len, _ = u.shape
    rows = min(PREP_ROWS, t_len)
    nb = rows // SUBLANES
    blk = lambda width: pl.BlockSpec((1, rows, width), lambda b, i: (b, i, 0))
    out_shapes = (
        jax.ShapeDtypeStruct((bsz, t_len, GDN_DIM), F32),
        jax.ShapeDtypeStruct((bsz, t_len, GDN_DIM), BF16),
        jax.ShapeDtypeStruct((bsz, t_len, GDN_DIM), BF16),
        jax.ShapeDtypeStruct((bsz, t_len, GDN_DIM), BF16),
        jax.ShapeDtypeStruct((bsz, GDN_HEADS, t_len, GDN_CHUNK), BF16),
        jax.ShapeDtypeStruct((bsz, t_len, LANES), F32),
    )
    return pl.pallas_call(
        _gdn_prep_kernel,
        grid=(bsz, t_len // rows),
        in_specs=[
            blk(CONV_DIM),
            pl.BlockSpec((1, SUBLANES, CONV_DIM), lambda b, i: (b, jnp.maximum(i * nb - 1, 0), 0)),
            blk(LANES),
            _const_spec((CONV_WIDTH, CONV_DIM)),
            _const_spec((1, LANES)),
            _const_spec((1, LANES)),
            _const_spec((GDN_CHUNK, GDN_CHUNK)),
        ],
        out_specs=(blk(GDN_DIM), blk(GDN_DIM), blk(GDN_DIM), blk(GDN_DIM),
                   pl.BlockSpec((1, GDN_HEADS, rows, GDN_CHUNK), lambda b, i: (b, 0, i, 0)),
                   blk(LANES)),
        out_shape=out_shapes,
        scratch_shapes=[pltpu.VMEM((rows + SUBLANES, CONV_DIM), F32)],
        compiler_params=_params("parallel", "parallel"),
        name="gdn_prep",
    )(u, u, ab, conv_w, alog_pad, dtb_pad, lincl)


def _gdn_scan_kernel(un_ref, wn_ref, qg_ref, kg_ref, att_ref, cg_ref, z_ref, gw_ref, o_ref, s_out_ref, s_ref):
    c = pl.program_id(0)
    bsz = un_ref.shape[0]
    c_len = GDN_CHUNK
    dk = GDN_HEAD_DIM

    @pl.when(c == 0)
    def _():
        s_ref[...] = jnp.zeros_like(s_ref)

    for b in range(bsz):
        decay_last = jnp.exp(cg_ref[b, c_len - 1:c_len, :])
        for h in range(GDN_HEADS):
            cols = slice(h * dk, (h + 1) * dk)
            s = s_ref[b, h]
            sb = s.astype(BF16)
            v_new = un_ref[b, :, cols] - _dot(wn_ref[b, :, cols], sb)
            vb = v_new.astype(BF16)
            o = _dot(qg_ref[b, :, cols], sb) + _dot(att_ref[b, h], vb)
            s_ref[b, h] = s * decay_last[:, h:h + 1] + _dot_tn(kg_ref[b, :, cols], vb)
            o_ref[b, :, cols] = _gated_out_norm(o, z_ref[b, :, cols], gw_ref[...]).astype(BF16)

    @pl.when(c == pl.num_programs(0) - 1)
    def _():
        s_out_ref[...] = s_ref[...]


def _gdn_scan(un, wn, qg, kg, att, cg, z, gw):
    bsz, t_len, _ = un.shape
    c_len = GDN_CHUNK
    blk = lambda width: pl.BlockSpec((bsz, c_len, width), lambda c: (0, c, 0))
    state_shape = (bsz, GDN_HEADS, GDN_HEAD_DIM, GDN_HEAD_DIM)
    return pl.pallas_call(
        _gdn_scan_kernel,
        grid=(t_len // c_len,),
        in_specs=[blk(GDN_DIM), blk(GDN_DIM), blk(GDN_DIM), blk(GDN_DIM),
                  pl.BlockSpec((bsz, GDN_HEADS, c_len, c_len), lambda c: (0, 0, c, 0)),
                  blk(LANES), blk(GDN_DIM), _const_spec((1, GDN_HEAD_DIM))],
        out_specs=(blk(GDN_DIM), pl.BlockSpec(state_shape, lambda c: (0, 0, 0, 0))),
        out_shape=(jax.ShapeDtypeStruct((bsz, t_len, GDN_DIM), BF16),
                   jax.ShapeDtypeStruct(state_shape, F32)),
        scratch_shapes=[pltpu.VMEM(state_shape, F32)],
        compiler_params=_params("arbitrary"),
        name="gdn_scan",
    )(un, wn, qg, kg, att, cg, z, gw)


def _gdn_decode_kernel(t_new, u_ref, hist_ref, ab_ref, z_ref, s_in_ref, cw_ref, alog_ref, dtb_ref, gw_ref,
                       o_ref, s_out_ref, ext_ref):
    dk = GDN_HEAD_DIM
    n_hist = CONV_WIDTH - 1
    for g in range(u_ref.shape[0]):
        ext_ref[...] = jnp.zeros_like(ext_ref)
        ext_ref[0:n_hist, :] = hist_ref[g]
        ext_ref[n_hist:n_hist + t_new, :] = u_ref[g]
        conv = _conv_silu(ext_ref, 0, SUBLANES, cw_ref)
        g_all, beta_all = _gate_terms(ab_ref[g], alog_ref, dtb_ref)
        decay_all = jnp.exp(g_all)
        tail = jnp.zeros((LANES - SUBLANES, dk), F32)
        for h in range(GDN_HEADS):
            q8 = _l2norm(conv[:, h * dk:(h + 1) * dk]) * (dk ** -0.5)
            k8 = _l2norm(conv[:, GDN_DIM + h * dk:GDN_DIM + (h + 1) * dk])
            v8 = conv[:, 2 * GDN_DIM + h * dk:2 * GDN_DIM + (h + 1) * dk]
            q_t = jnp.concatenate([q8, tail], axis=0).T
            k_t = jnp.concatenate([k8, tail], axis=0).T
            s = s_in_ref[g, h]
            for t in range(t_new):
                k_col = k_t[:, t:t + 1]
                s = s * decay_all[t:t + 1, h:h + 1]
                ks = jnp.sum(k_col * s, axis=0, keepdims=True)
                delta = beta_all[t:t + 1, GDN_HEADS + h:GDN_HEADS + h + 1] * (v8[t:t + 1, :] - ks)
                s = s + k_col * delta
                o = jnp.sum(q_t[:, t:t + 1] * s, axis=0, keepdims=True)
                o = _gated_out_norm(o, z_ref[g, t:t + 1, h * dk:(h + 1) * dk], gw_ref[...])
                o_ref[g, t:t + 1, h * dk:(h + 1) * dk] = o.astype(BF16)
            s_out_ref[g, h] = s


def _gdn_decode(u, hist, ab, z, s_in, conv_w, alog_pad, dtb_pad, gw):
    n_seq, t_new, _ = u.shape
    grp = DEC_GROUP
    seq = lambda r, width: pl.BlockSpec((grp, r, width), lambda i: (i, 0, 0))
    state = pl.BlockSpec((grp, GDN_HEADS, GDN_HEAD_DIM, GDN_HEAD_DIM), lambda i: (i, 0, 0, 0))
    return pl.pallas_call(
        functools.partial(_gdn_decode_kernel, t_new),
        grid=(n_seq // grp,),
        in_specs=[seq(t_new, CONV_DIM), seq(CONV_WIDTH - 1, CONV_DIM), seq(t_new, LANES), seq(t_new, GDN_DIM),
                  state, _const_spec((CONV_WIDTH, CONV_DIM)), _const_spec((1, LANES)), _const_spec((1, LANES)),
                  _const_spec((1, GDN_HEAD_DIM))],
        out_specs=(seq(t_new, GDN_DIM), state),
        out_shape=(jax.ShapeDtypeStruct((n_seq, t_new, GDN_DIM), BF16),
                   jax.ShapeDtypeStruct(s_in.shape, F32)),
        scratch_shapes=[pltpu.VMEM((2 * SUBLANES, CONV_DIM), F32)],
        compiler_params=_params("parallel"),
        name="gdn_decode",
    )(u, hist, ab, z, s_in, conv_w, alog_pad, dtb_pad, gw)


def _merge_kernel(x_ref, oa_ref, ob_ref, sg_ref, wpa_ref, wpb_ref, wo_ref, nw_ref, x1_ref, hm_ref):
    y_a = _dot(oa_ref[...], wpa_ref[...])
    y_b = _dot(ob_ref[...], wpb_ref[...])
    mix = sg_ref[:, :D_MODEL] * y_a + sg_ref[:, D_MODEL:] * y_b
    x1 = x_ref[...] + _dot(mix.astype(BF16), wo_ref[...])
    x1_ref[...] = x1
    hm_ref[...] = _rmsnorm(x1, nw_ref[...]).astype(BF16)


def _merge(x2d, o_a, o_b, sg, w_pa, w_pb, w_o, norm_w, tm):
    n = x2d.shape[0]
    row = lambda width: pl.BlockSpec((tm, width), lambda i: (i, 0))
    return pl.pallas_call(
        _merge_kernel,
        grid=(n // tm,),
        in_specs=[row(D_MODEL), row(SB_DIM), row(GDN_DIM), row(2 * D_MODEL),
                  _const_spec(w_pa.shape), _const_spec(w_pb.shape), _const_spec(w_o.shape),
                  _const_spec((1, D_MODEL))],
        out_specs=(row(D_MODEL), row(D_MODEL)),
        out_shape=(jax.ShapeDtypeStruct((n, D_MODEL), F32), jax.ShapeDtypeStruct((n, D_MODEL), BF16)),
        compiler_params=_params("parallel"),
        name="merge",
    )(x2d, o_a, o_b, sg, w_pa, w_pb, w_o, norm_w)


def _mlp_kernel(x1_ref, hm_ref, wup_ref, wdown_ref, nf_ref, y_ref, acc_ref):
    f = pl.program_id(1)

    @pl.when(f == 0)
    def _():
        acc_ref[...] = x1_ref[...]

    up = jnp.maximum(_dot(hm_ref[...], wup_ref[...]), 0.0)
    acc_ref[...] += _dot((up * up).astype(BF16), wdown_ref[...])

    @pl.when(f == pl.num_programs(1) - 1)
    def _():
        y_ref[...] = _rmsnorm(acc_ref[...], nf_ref[...])


def _mlp(x1, hm, w_up, w_down, norm_f, tm, tf):
    n = x1.shape[0]
    return pl.pallas_call(
        _mlp_kernel,
        grid=(n // tm, D_FF // tf),
        in_specs=[pl.BlockSpec((tm, D_MODEL), lambda i, f: (i, 0)),
                  pl.BlockSpec((tm, D_MODEL), lambda i, f: (i, 0)),
                  pl.BlockSpec((D_MODEL, tf), lambda i, f: (0, f)),
                  pl.BlockSpec((tf, D_MODEL), lambda i, f: (f, 0)),
                  _const_spec((1, D_MODEL))],
        out_specs=pl.BlockSpec((tm, D_MODEL), lambda i, f: (i, 0)),
        out_shape=jax.ShapeDtypeStruct((n, D_MODEL), F32),
        scratch_shapes=[pltpu.VMEM((tm, D_MODEL), F32)],
        compiler_params=_params("parallel", "arbitrary"),
        name="mlp",
    )(x1, hm, w_up, w_down, norm_f)


def _permute_w_in(w):
    n_main = 3 * SB_DIM + CONV_DIM + GDN_DIM
    gates = w[:, n_main + 2 * GDN_HEADS:]
    ab = w[:, n_main:n_main + 2 * GDN_HEADS]
    pad = jnp.zeros((w.shape[0], LANES - 2 * GDN_HEADS), w.dtype)
    return jnp.concatenate([w[:, :n_main], gates, ab, pad], axis=1).astype(BF16)


def _lane_pad(vec, offset):
    return jnp.zeros((1, LANES), F32).at[0, offset:offset + vec.shape[0]].set(vec.astype(F32))


def _layer_common(x2d, lw, tm):
    return _inproj(x2d, lw["norm_mix_w"], lw["w_in"], tm)


def _layer_tail(x2d, o_a, o_b, sg, lw, norm_f, tm, tm_mlp, tf):
    x1, hm = _merge(x2d, o_a, o_b, sg, lw["w_pa"], lw["w_pb"], lw["w_o"], lw["norm_mlp_w"], tm)
    return _mlp(x1, hm, lw["w_up"], lw["w_down"], norm_f, tm_mlp, tf)


def kernel(x_prompt, x_sample, cache_k, cache_v, page_table, state_conv, state_ssm, norm_mix_w, w_in, sb_bias,
           conv_w, a_log, dt_bias, gdn_norm_w, w_pa, w_pb, w_o, norm_mlp_w, w_up, w_down, norm_final_w):
    depth = w_in.shape[0]
    assert depth == 1, "the residual stream is normalised once, after the only layer"
    b_p, t_p, _ = x_prompt.shape
    b_s, t_s, _ = x_sample.shape
    n_pool, page = cache_k.shape[1], cache_k.shape[2]
    l = 0
    lw = {
        "norm_mix_w": norm_mix_w[l].reshape(1, D_MODEL).astype(F32),
        "w_in": _permute_w_in(w_in[l]),
        "w_pa": w_pa[l].astype(BF16), "w_pb": w_pb[l].astype(BF16), "w_o": w_o[l].astype(BF16),
        "norm_mlp_w": norm_mlp_w[l].reshape(1, D_MODEL).astype(F32),
        "w_up": w_up[l].astype(BF16), "w_down": w_down[l].astype(BF16),
    }
    norm_f = norm_final_w.reshape(1, D_MODEL).astype(F32)
    bias = sb_bias[l].astype(F32)
    cw = conv_w[l].astype(F32)
    alog_pad = _lane_pad(a_log[l], 0)
    dtb_pad = _lane_pad(dt_bias[l], 0)
    gw = gdn_norm_w[l].reshape(1, GDN_HEAD_DIM).astype(F32)
    tri = jnp.tril(jnp.ones((SB_BLOCK, SB_BLOCK), BF16), -1)
    lincl = jnp.tril(jnp.ones((GDN_CHUNK, GDN_CHUNK), BF16))

    xp = x_prompt.reshape(b_p * t_p, D_MODEL).astype(F32)
    q, k, v, kb, vb, u, z, ab, sg = _layer_common(xp, lw, 256)
    shp = lambda a: a.reshape(b_p, t_p, a.shape[-1])
    o_a = _sb_prompt(bias, shp(q), shp(kb), shp(vb), tri)
    un, wn, qg, kg, att, cg = _gdn_prep(shp(u), shp(ab), cw, alog_pad, dtb_pad, lincl)
    o_b, ssm_p = _gdn_scan(un, wn, qg, kg, att, cg, shp(z), gw)
    y_p = _layer_tail(xp, o_a.reshape(-1, SB_DIM), o_b.reshape(-1, GDN_DIM), sg, lw, norm_f, 512, 1024, 512)
    y_prompt = y_p.reshape(b_p, t_p, D_MODEL).astype(x_prompt.dtype)
    page_shape = (depth, b_p, t_p // page, page, SB_HEADS, SB_HEAD_DIM)
    new_k_prompt = k.reshape(page_shape).astype(cache_k.dtype)
    new_v_prompt = v.reshape(page_shape).astype(cache_v.dtype)
    new_conv_prompt = shp(u)[None, :, t_p - (CONV_WIDTH - 1):, :].astype(state_conv.dtype)
    new_ssm_prompt = ssm_p[None].astype(state_ssm.dtype)

    xs = x_sample.reshape(b_s * t_s, D_MODEL).astype(F32)
    tm_s = min(256, b_s * t_s)
    q, k, v, _, _, u, z, ab, sg = _layer_common(xs, lw, tm_s)
    shs = lambda a: a.reshape(b_s, t_s, a.shape[-1])
    pad8 = lambda a: jnp.pad(shs(a), ((0, 0), (0, SUBLANES - t_s), (0, 0)))
    pages_t = lambda c: jnp.transpose(c[l], (0, 2, 3, 1)).reshape(n_pool, SB_DIM, page).astype(F32)
    o_a = _sb_decode(page_table, bias, shs(q), pad8(k), pad8(v), tri, pages_t(cache_k), pages_t(cache_v))
    o_b, ssm_s = _gdn_decode(shs(u), state_conv[l].astype(F32), shs(ab), shs(z), state_ssm[l].astype(F32),
                             cw, alog_pad, dtb_pad, gw)
    y_s = _layer_tail(xs, o_a.reshape(-1, SB_DIM), o_b.reshape(-1, GDN_DIM), sg, lw, norm_f, tm_s, tm_s, 512)
    y_sample = y_s.reshape(b_s, t_s, D_MODEL).astype(x_sample.dtype)
    head_shape = (depth, b_s, t_s, SB_HEADS, SB_HEAD_DIM)
    new_k_sample = k.reshape(head_shape).astype(cache_k.dtype)
    new_v_sample = v.reshape(head_shape).astype(cache_v.dtype)
    new_conv_sample = jnp.concatenate([state_conv[l].astype(F32), shs(u)], axis=1)[None, :, t_s:, :].astype(state_conv.dtype)
    new_ssm_sample = ssm_s[None].astype(state_ssm.dtype)

    return (y_prompt, y_sample, new_k_prompt, new_v_prompt, new_k_sample, new_v_sample,
            new_conv_prompt, new_conv_sample, new_ssm_prompt, new_ssm_sample)
```

```python
import functools
import math

import jax
import jax.numpy as jnp
from jax import lax
from jax.experimental import pallas as pl
from jax.experimental.pallas import tpu as pltpu

F32 = jnp.float32
BF16 = jnp.bfloat16

D_MODEL = 1024
SB_HEADS = 8
SB_HEAD_DIM = 64
SB_DIM = SB_HEADS * SB_HEAD_DIM
GDN_HEADS = 4
GDN_HEAD_DIM = 128
GDN_DIM = GDN_HEADS * GDN_HEAD_DIM
CONV_WIDTH = 4
CONV_DIM = 3 * GDN_DIM
GDN_CHUNK = 64
D_FF = 4 * D_MODEL
NORM_EPS = 1e-6
L2_EPS = 1e-6
NEG_LOG2E = -1.4426950408889634

LANES = 128
SUBLANES = 8
VMEM_LIMIT_BYTES = 56 * 1024 * 1024

COL_Q = 0
COL_K = COL_Q + SB_DIM
COL_V = COL_K + SB_DIM
COL_U = COL_V + SB_DIM
COL_Z = COL_U + CONV_DIM
COL_GA = COL_Z + GDN_DIM
COL_GB = COL_GA + D_MODEL
COL_AB = COL_GB + D_MODEL
IN_COLS = COL_AB + LANES

SB_BLOCK = 256
PREP_ROWS = 512
DEC_GROUP = 8


def _dot(a, b):
    return jnp.dot(a, b, preferred_element_type=F32)


def _dot_nt(a, b):
    return lax.dot_general(a, b, (((1,), (1,)), ((), ())), preferred_element_type=F32)


def _dot_tn(a, b):
    return lax.dot_general(a, b, (((0,), (0,)), ((), ())), preferred_element_type=F32)


def _dot_exact01(m01, x):
    h = x.astype(BF16)
    r = x - h.astype(F32)
    m = r.astype(BF16)
    l = (r - m.astype(F32)).astype(BF16)
    return _dot(m01, h) + (_dot(m01, m) + _dot(m01, l))


def _softplus(z):
    return jnp.maximum(z, 0.0) + jnp.log(1.0 + jnp.exp2(jnp.abs(z) * NEG_LOG2E))


def _sigmoid(z):
    return 1.0 / (1.0 + jnp.exp(-z))


def _silu(z):
    return z * _sigmoid(z)


def _rmsnorm(x, w):
    return x * lax.rsqrt(jnp.mean(x * x, axis=-1, keepdims=True) + NORM_EPS) * w


def _const_spec(shape):
    nd = len(shape)
    return pl.BlockSpec(shape, lambda *_: (0,) * nd, pipeline_mode=pl.Buffered(1))


def _params(*sem):
    return pltpu.CompilerParams(dimension_semantics=sem, vmem_limit_bytes=VMEM_LIMIT_BYTES)


def _inproj_kernel(x_ref, nw_ref, w_ref, q_ref, k_ref, v_ref, kb_ref, vb_ref, u_ref, z_ref, ab_ref, sg_ref):
    x = x_ref[...]
    hb = _rmsnorm(x, nw_ref[...]).astype(BF16)

    def seg(lo, width):
        return _dot(hb, w_ref[:, lo:lo + width])

    q_ref[...] = (seg(COL_Q, SB_DIM) * (SB_HEAD_DIM ** -0.5)).astype(BF16)
    k = seg(COL_K, SB_DIM)
    k_ref[...] = k
    kb_ref[...] = k.astype(BF16)
    v = seg(COL_V, SB_DIM)
    v_ref[...] = v
    vb_ref[...] = v.astype(BF16)
    for j in range(CONV_DIM // SB_DIM):
        u_ref[:, j * SB_DIM:(j + 1) * SB_DIM] = seg(COL_U + j * SB_DIM, SB_DIM)
    z_ref[...] = seg(COL_Z, GDN_DIM)
    for j in range(2 * D_MODEL // SB_DIM):
        sg_ref[:, j * SB_DIM:(j + 1) * SB_DIM] = _sigmoid(seg(COL_GA + j * SB_DIM, SB_DIM))
    ab_ref[...] = seg(COL_AB, LANES)


def _inproj(x2d, norm_w, w_perm, tm):
    n = x2d.shape[0]
    row = lambda width: pl.BlockSpec((tm, width), lambda i: (i, 0))
    out_shapes = (
        jax.ShapeDtypeStruct((n, SB_DIM), BF16),
        jax.ShapeDtypeStruct((n, SB_DIM), F32),
        jax.ShapeDtypeStruct((n, SB_DIM), F32),
        jax.ShapeDtypeStruct((n, SB_DIM), BF16),
        jax.ShapeDtypeStruct((n, SB_DIM), BF16),
        jax.ShapeDtypeStruct((n, CONV_DIM), F32),
        jax.ShapeDtypeStruct((n, GDN_DIM), F32),
        jax.ShapeDtypeStruct((n, LANES), F32),
        jax.ShapeDtypeStruct((n, 2 * D_MODEL), F32),
    )
    return pl.pallas_call(
        _inproj_kernel,
        grid=(n // tm,),
        in_specs=[row(D_MODEL), _const_spec((1, D_MODEL)), _const_spec((D_MODEL, IN_COLS))],
        out_specs=tuple(row(s.shape[1]) for s in out_shapes),
        out_shape=out_shapes,
        compiler_params=_params("parallel"),
        name="inproj",
    )(x2d, norm_w, w_perm)


def _sb_tile(z, carry, ntri, mask):
    sp = _softplus(z)
    if mask is not None:
        sp = jnp.where(mask, sp, 0.0)
    suffix = _dot(sp.astype(BF16), ntri)
    w = jnp.exp(z - sp + suffix + carry)
    if mask is not None:
        w = jnp.where(mask, w, 0.0)
    return w.astype(BF16), carry - jnp.sum(sp, axis=1, keepdims=True)


def _sb_prompt_kernel(bias_ref, q_ref, k_ref, v_ref, ntri_ref, o_ref,
                      qs_scr, bias_scr, z_scr, lb_scr, sp_scr, w_scr, carry_scr, run_scr, acc_scr):
    hp = pl.program_id(1)
    tb = SB_BLOCK
    n_q = q_ref.shape[1] // tb
    n_tiles = n_q * (n_q + 1) // 2
    lane = lax.broadcasted_iota(jnp.int32, (1, LANES), 1)
    head_lanes = [lane < SB_HEAD_DIM, lane >= SB_HEAD_DIM]
    zero = jnp.zeros((), BF16)
    rows = [slice(j * tb, (j + 1) * tb) for j in range(2)]

    for ref in (z_scr, lb_scr, sp_scr, w_scr, carry_scr, run_scr, acc_scr):
        ref[...] = jnp.zeros_like(ref)
    row = lax.broadcasted_iota(jnp.int32, (tb, tb), 0)
    col = lax.broadcasted_iota(jnp.int32, (tb, tb), 1)
    for j in range(2):
        bias = bias_ref[2 * hp + j]
        bias_scr[0, j] = jnp.full((tb, tb), bias, F32)
        bias_scr[1, j] = jnp.where(col < row, bias, -1e30)
    for i in range(n_q):
        q = q_ref[0, i * tb:(i + 1) * tb, :]
        for j in range(2):
            qs_scr[i, rows[j], :] = jnp.where(head_lanes[j], q, zero)

    def key_block(ref, kb):
        return ref[0, pl.ds(pl.multiple_of(kb * tb, tb), tb), :]

    def step(s, tiles):
        (q_a, k_a), (q_b, k_b), (q_d, k_d) = tiles[0], tiles[1], tiles[3]
        valid_d = jnp.logical_and(s >= 3, s - 3 < n_tiles)
        v = key_block(v_ref, k_d)
        pv = None
        for j in range(2):
            vm = jnp.where(jnp.logical_and(head_lanes[j], valid_d), v, zero)
            term = _dot(w_scr[rows[j], :], vm)
            pv = term if pv is None else pv + term
        acc_rows = pl.ds(pl.multiple_of(q_d * tb, tb), tb)
        acc_scr[acc_rows, :] = acc_scr[acc_rows, :] + pv
        suffix = _dot(sp_scr[...], ntri_ref[...])
        for j in range(2):
            w = jnp.exp(lb_scr[rows[j], :] + suffix[rows[j], :] + carry_scr[rows[j], :])
            w_scr[rows[j], :] = w.astype(BF16)
        diagonal = k_b == q_b
        for j in range(2):
            z = z_scr[rows[j], :] + bias_scr[diagonal.astype(jnp.int32), j]
            sp = _softplus(z)
            lb_scr[rows[j], :] = z - sp
            sp_scr[rows[j], :] = sp.astype(BF16)
            run = jnp.where(diagonal, 0.0, run_scr[rows[j], :])
            carry_scr[rows[j], :] = run
            run_scr[rows[j], :] = run - jnp.sum(sp, axis=1, keepdims=True)
        z_scr[...] = _dot_nt(qs_scr[q_a], key_block(k_ref, k_a))
        wrap = k_a == 0
        q_n = jnp.where(wrap, jnp.minimum(q_a + 1, n_q - 1), q_a)
        k_n = jnp.where(wrap, q_n, k_a - 1)
        return ((q_n, k_n),) + tiles[:3]

    first = (jnp.int32(0), jnp.int32(0))
    lax.fori_loop(0, n_tiles + 3, step, (first,) * 4)
    o_ref[0] = acc_scr[...].astype(BF16)


def _sb_prompt(sb_bias, q, k, v, ntri):
    bsz, t_len, _ = q.shape
    tb = SB_BLOCK
    seq = pl.BlockSpec((1, t_len, LANES), lambda b, h, *_: (b, 0, h))
    return pl.pallas_call(
        _sb_prompt_kernel,
        grid_spec=pltpu.PrefetchScalarGridSpec(
            num_scalar_prefetch=1,
            grid=(bsz, SB_DIM // LANES),
            in_specs=[seq, seq, seq, pl.BlockSpec((tb, tb), lambda b, h, *_: (0, 0))],
            out_specs=seq,
            scratch_shapes=[
                pltpu.VMEM((t_len // tb, 2 * tb, LANES), BF16),
                pltpu.VMEM((2, 2, tb, tb), F32),
                pltpu.VMEM((2 * tb, tb), F32),
                pltpu.VMEM((2 * tb, tb), F32),
                pltpu.VMEM((2 * tb, tb), BF16),
                pltpu.VMEM((2 * tb, tb), BF16),
                pltpu.VMEM((2 * tb, 1), F32),
                pltpu.VMEM((2 * tb, 1), F32),
                pltpu.VMEM((t_len, LANES), F32),
            ],
        ),
        out_shape=jax.ShapeDtypeStruct((bsz, t_len, SB_DIM), BF16),
        compiler_params=_params("parallel", "parallel"),
        name="sb_prompt",
    )(sb_bias, q, k, v, ntri)


def _sb_decode_kernel(n_pages, t_new, pt_ref, bias_ref, q_ref, kn_ref, vn_ref, ntri_ref, *refs):
    kt_pages = refs[:n_pages]
    vt_pages = refs[n_pages:2 * n_pages]
    o_ref = refs[2 * n_pages]
    page = kt_pages[0].shape[2]
    rows = t_new * SB_HEADS
    r_iota = lax.broadcasted_iota(jnp.int32, (rows, SB_DIM), 0)
    l_iota = lax.broadcasted_iota(jnp.int32, (rows, SB_DIM), 1)
    head_of_row = r_iota % SB_HEADS
    own_lanes = (l_iota // SB_HEAD_DIM) == head_of_row

    q = q_ref[0].astype(F32)
    q_rep = jnp.broadcast_to(q[:, None, :], (t_new, SB_HEADS, SB_DIM)).reshape(rows, SB_DIM)
    q_bd = jnp.where(own_lanes, q_rep, 0.0).astype(BF16)

    r1 = lax.broadcasted_iota(jnp.int32, (rows, 1), 0)
    bias = jnp.zeros((rows, 1), F32)
    for h in range(SB_HEADS):
        bias = jnp.where(r1 % SB_HEADS == h, bias_ref[h], bias)

    pad = jnp.zeros((page - SUBLANES, SB_DIM), F32)
    k_new = jnp.concatenate([kn_ref[0], pad], axis=0).astype(BF16)
    v_new = jnp.concatenate([vn_ref[0], pad], axis=0).astype(BF16)
    kcol = lax.broadcasted_iota(jnp.int32, (rows, page), 1)
    qtok = lax.broadcasted_iota(jnp.int32, (rows, page), 0) // SB_HEADS
    z = _dot_nt(q_bd, k_new) + bias
    w, carry = _sb_tile(z, jnp.zeros((rows, 1), F32), ntri_ref[:page, :page], kcol < qtok)
    acc = _dot(w, v_new)

    blk = ntri_ref.shape[0] // page
    order = range(n_pages // blk - 1, -1, -1)
    lane_cat = lambda pages, pb: jnp.concatenate(
        [pages[pb * blk + i][0] for i in range(blk)], axis=1).astype(BF16)
    zs = [_dot(q_bd, lane_cat(kt_pages, pb)) + bias for pb in order]
    sps = [_softplus(z) for z in zs]
    suffixes = [_dot(sp.astype(BF16), ntri_ref[...]) for sp in sps]
    for pb, z, sp, suffix in zip(order, zs, sps, suffixes):
        w = jnp.exp(z - sp + suffix + carry).astype(BF16)
        carry = carry - jnp.sum(sp, axis=1, keepdims=True)
        acc = acc + _dot_nt(w, lane_cat(vt_pages, pb))

    acc = jnp.where(own_lanes, acc, 0.0)
    o_ref[0] = jnp.sum(acc.reshape(t_new, SB_HEADS, SB_DIM), axis=1).astype(BF16)


def _sb_decode(page_table, sb_bias, q, k_new8, v_new8, ntri, cache_kt, cache_vt):
    n_seq, t_new, _ = q.shape
    n_pages = page_table.shape[1]
    page = cache_kt.shape[2]

    def page_spec(p):
        return pl.BlockSpec((1, SB_DIM, page), lambda s, pt, b: (pt[s, p], 0, 0))

    seq_spec = lambda r: pl.BlockSpec((1, r, SB_DIM), lambda s, pt, b: (s, 0, 0))
    return pl.pallas_call(
        functools.partial(_sb_decode_kernel, n_pages, t_new),
        grid_spec=pltpu.PrefetchScalarGridSpec(
            num_scalar_prefetch=2,
            grid=(n_seq,),
            in_specs=[seq_spec(t_new), seq_spec(SUBLANES), seq_spec(SUBLANES),
                      pl.BlockSpec(ntri.shape, lambda s, pt, b: (0, 0))]
                     + [page_spec(p) for p in range(n_pages)] * 2,
            out_specs=seq_spec(t_new),
        ),
        out_shape=jax.ShapeDtypeStruct((n_seq, t_new, SB_DIM), BF16),
        compiler_params=_params("parallel"),
        name="sb_decode",
    )(page_table, sb_bias, q, k_new8, v_new8, ntri, *([cache_kt] * n_pages), *([cache_vt] * n_pages))


def _conv_silu(ext_ref, first, rows, cw_ref):
    acc = None
    for w in range(CONV_WIDTH):
        term = ext_ref[first + w:first + w + rows, :] * cw_ref[w:w + 1, :]
        acc = term if acc is None else acc + term
    return _silu(acc)


def _l2norm(x):
    return x * lax.rsqrt(jnp.sum(x * x, axis=-1, keepdims=True) + L2_EPS)


def _gate_terms(ab, alog_ref, dtb_ref):
    g = -jnp.exp(alog_ref[...]) * _softplus(ab + dtb_ref[...])
    return g, _sigmoid(ab)


def _gated_out_norm(o, z, gw):
    o = o * lax.rsqrt(jnp.mean(o * o, axis=-1, keepdims=True) + NORM_EPS) * gw
    return o * _silu(z)


def _gdn_prep_kernel(u_ref, prev_ref, ab_ref, cw_ref, alog_ref, dtb_ref, lincl_ref,
                     un_ref, wn_ref, qg_ref, kg_ref, att_ref, cg_ref, ext_ref):
    rows = u_ref.shape[1]
    c_len = GDN_CHUNK
    dk = GDN_HEAD_DIM
    first_block = pl.program_id(1) == 0
    prev = prev_ref[0]
    ext_ref[0:SUBLANES, :] = jnp.where(first_block, jnp.zeros_like(prev), prev)
    ext_ref[SUBLANES:, :] = u_ref[0]
    conv = _conv_silu(ext_ref, SUBLANES - (CONV_WIDTH - 1), rows, cw_ref)

    g_all, beta_all = _gate_terms(ab_ref[0], alog_ref, dtb_ref)
    r_i = lax.broadcasted_iota(jnp.int32, (c_len, c_len), 0)
    c_i = lax.broadcasted_iota(jnp.int32, (c_len, c_len), 1)
    incl = c_i <= r_i
    strict = c_i < r_i
    lincl = lincl_ref[...]
    n_chunks = rows // c_len

    cg_chunks = [_dot_exact01(lincl, g_all[c * c_len:(c + 1) * c_len, :]) for c in range(n_chunks)]
    cg_t_pairs = []
    for pair in range(n_chunks // 2):
        cg_pair = jnp.concatenate(cg_chunks[2 * pair:2 * pair + 2], axis=0)
        cg_ref[0, 2 * pair * c_len:(2 * pair + 2) * c_len, :] = cg_pair
        cg_t_pairs.append(cg_pair.T)

    problems = [(c, h) for c in range(n_chunks) for h in range(GDN_HEADS)]

    def head_cols(base, h):
        return slice(base + h * dk, base + (h + 1) * dk)

    qs, ks, vs, bcs, cgcols, decays = [], [], [], [], [], []
    for c, h in problems:
        r = slice(c * c_len, (c + 1) * c_len)
        qs.append(_l2norm(conv[r, head_cols(0, h)]) * (dk ** -0.5))
        ks.append(_l2norm(conv[r, head_cols(GDN_DIM, h)]))
        vs.append(conv[r, head_cols(2 * GDN_DIM, h)])
        bcs.append(beta_all[r, GDN_HEADS + h:GDN_HEADS + h + 1])
        cg_col = cg_chunks[c][:, h:h + 1]
        cg_row = cg_t_pairs[c // 2][h:h + 1, (c % 2) * c_len:(c % 2 + 1) * c_len]
        cgcols.append(cg_col)
        decays.append(jnp.exp(jnp.where(incl, cg_col - cg_row, -jnp.inf)))
    kbs = [k.astype(BF16) for k in ks]
    kks = [_dot_nt(kb, kb) for kb in kbs]

    powers = [jnp.where(strict, -(bc * kk * dec), 0.0) for bc, kk, dec in zip(bcs, kks, decays)]
    inv_off = powers
    for _ in range(int(math.log2(c_len)) - 1):
        pbs = [p.astype(BF16) for p in powers]
        powers = [_dot(pb, pb) for pb in pbs]
        next_pbs = [p.astype(BF16) for p in powers]
        inv_off = [t + p + _dot(t.astype(BF16), pb) for t, p, pb in zip(inv_off, powers, next_pbs)]

    for (c, h), q, k, v, bc, cg_col, dec, t_off in zip(problems, qs, ks, vs, bcs, cgcols, decays, inv_off):
        r = slice(c * c_len, (c + 1) * c_len)
        cols = head_cols(0, h)
        rhs = jnp.concatenate([v * bc, k * (bc * jnp.exp(cg_col))], axis=1)
        sol = rhs + _dot(t_off.astype(BF16), rhs.astype(BF16))
        un_ref[0, r, cols] = sol[:, :dk]
        wn_ref[0, r, cols] = sol[:, dk:].astype(BF16)
        qg_ref[0, r, cols] = (q * jnp.exp(cg_col)).astype(BF16)
        g_last = cg_chunks[c][c_len - 1:c_len, h:h + 1]
        kg_ref[0, r, cols] = (k * jnp.exp(g_last - cg_col)).astype(BF16)
        qk = _dot_nt(q.astype(BF16), k.astype(BF16))
        att_ref[0, h, r, :] = (qk * dec).astype(BF16)


def _gdn_prep(u, ab, conv_w, alog_pad, dtb_pad, lincl):
    bsz, t_len, _ = u.shape
    rows = min(PREP_ROWS, t_len)
    nb = rows // SUBLANES
    blk = lambda width: pl.BlockSpec((1, rows, width), lambda b, i: (b, i, 0))
    out_shapes = (
        jax.ShapeDtypeStruct((bsz, t_len, GDN_DIM), F32),
        jax.ShapeDtypeStruct((bsz, t_len, GDN_DIM), BF16),
        jax.ShapeDtypeStruct((bsz, t_len, GDN_DIM), BF16),
        jax.ShapeDtypeStruct((bsz, t_len, GDN_DIM), BF16),
        jax.ShapeDtypeStruct((bsz, GDN_HEADS, t_len, GDN_CHUNK), BF16),
        jax.ShapeDtypeStruct((bsz, t_len, LANES), F32),
    )
    return pl.pallas_call(
        _gdn_prep_kernel,
        grid=(bsz, t_len // rows),
        in_specs=[
            blk(CONV_DIM),
            pl.BlockSpec((1, SUBLANES, CONV_DIM), lambda b, i: (b, jnp.maximum(i * nb - 1, 0), 0)),
            blk(LANES),
            _const_spec((CONV_WIDTH, CONV_DIM)),
            _const_spec((1, LANES)),
            _const_spec((1, LANES)),
            _const_spec((GDN_CHUNK, GDN_CHUNK)),
        ],
        out_specs=(blk(GDN_DIM), blk(GDN_DIM), blk(GDN_DIM), blk(GDN_DIM),
                   pl.BlockSpec((1, GDN_HEADS, rows, GDN_CHUNK), lambda b, i: (b, 0, i, 0)),
                   blk(LANES)),
        out_shape=out_shapes,
        scratch_shapes=[pltpu.VMEM((rows + SUBLANES, CONV_DIM), F32)],
        compiler_params=_params("parallel", "parallel"),
        name="gdn_prep",
    )(u, u, ab, conv_w, alog_pad, dtb_pad, lincl)


def _gdn_scan_kernel(un_ref, wn_ref, qg_ref, kg_ref, att_ref, cg_ref, z_ref, gw_ref, o_ref, s_out_ref, s_ref):
    c = pl.program_id(0)
    bsz = un_ref.shape[0]
    c_len = GDN_CHUNK
    dk = GDN_HEAD_DIM

    @pl.when(c == 0)
    def _():
        s_ref[...] = jnp.zeros_like(s_ref)

    chains = [(b, h, slice(h * dk, (h + 1) * dk)) for b in range(bsz) for h in range(GDN_HEADS)]
    decay_last = [jnp.exp(cg_ref[b, c_len - 1:c_len, :]) for b in range(bsz)]
    states = [s_ref[b, h] for b, h, _ in chains]
    sbs = [s.astype(BF16) for s in states]
    ws = [_dot(wn_ref[b, :, cols], sb) for (b, _, cols), sb in zip(chains, sbs)]
    qs = [_dot(qg_ref[b, :, cols], sb) for (b, _, cols), sb in zip(chains, sbs)]
    vbs = [(un_ref[b, :, cols] - w).astype(BF16) for (b, _, cols), w in zip(chains, ws)]
    for (b, h, cols), s, q_s, vb in zip(chains, states, qs, vbs):
        o = q_s + _dot(att_ref[b, h], vb)
        s_ref[b, h] = s * decay_last[b][:, h:h + 1] + _dot_tn(kg_ref[b, :, cols], vb)
        o_ref[b, :, cols] = _gated_out_norm(o, z_ref[b, :, cols], gw_ref[...]).astype(BF16)

    @pl.when(c == pl.num_programs(0) - 1)
    def _():
        s_out_ref[...] = s_ref[...]


def _gdn_scan(un, wn, qg, kg, att, cg, z, gw):
    bsz, t_len, _ = un.shape
    c_len = GDN_CHUNK
    blk = lambda width: pl.BlockSpec((bsz, c_len, width), lambda c: (0, c, 0))
    state_shape = (bsz, GDN_HEADS, GDN_HEAD_DIM, GDN_HEAD_DIM)
    return pl.pallas_call(
        _gdn_scan_kernel,
        grid=(t_len // c_len,),
        in_specs=[blk(GDN_DIM), blk(GDN_DIM), blk(GDN_DIM), blk(GDN_DIM),
                  pl.BlockSpec((bsz, GDN_HEADS, c_len, c_len), lambda c: (0, 0, c, 0)),
                  blk(LANES), blk(GDN_DIM), _const_spec((1, GDN_HEAD_DIM))],
        out_specs=(blk(GDN_DIM), pl.BlockSpec(state_shape, lambda c: (0, 0, 0, 0))),
        out_shape=(jax.ShapeDtypeStruct((bsz, t_len, GDN_DIM), BF16),
                   jax.ShapeDtypeStruct(state_shape, F32)),
        scratch_shapes=[pltpu.VMEM(state_shape, F32)],
        compiler_params=_params("arbitrary"),
        name="gdn_scan",
    )(un, wn, qg, kg, att, cg, z, gw)


def _gdn_decode_kernel(t_new, u_ref, hist_ref, ab_ref, z_ref, s_in_ref, cw_ref, alog_ref, dtb_ref, gw_ref,
                       o_ref, s_out_ref, ext_ref, ab_scr):
    dk = GDN_HEAD_DIM
    n_hist = CONV_WIDTH - 1
    n_seq = u_ref.shape[0]
    tile = (SUBLANES, LANES)
    tail = jnp.zeros((LANES - SUBLANES, LANES), F32)
    pad_rows = lambda x: jnp.concatenate([x, tail], axis=0)
    row = lax.broadcasted_iota(jnp.int32, tile, 0)
    lane = lax.broadcasted_iota(jnp.int32, tile, 1)
    real = row < t_new
    ext_ref[...] = jnp.zeros_like(ext_ref)
    ab_scr[...] = jnp.zeros_like(ab_scr)

    problems = [(g, h) for g in range(n_seq) for h in range(GDN_HEADS)]
    q8, k8, v8, e_col, b_col, w_col, decay, e_last = {}, {}, {}, {}, {}, {}, {}, {}
    for g in range(n_seq):
        ext_ref[g, 0:n_hist, :] = hist_ref[g]
        ext_ref[g, n_hist:n_hist + t_new, :] = u_ref[g]
        ab_scr[g, 0:t_new, :] = ab_ref[g]
        conv = _conv_silu(ext_ref.at[g], 0, SUBLANES, cw_ref)
        g_all, beta_all = _gate_terms(ab_scr[g], alog_ref, dtb_ref)
        cg = g_all
        for shift in range(1, t_new):
            cg = cg + jnp.where(row >= shift, pltpu.roll(g_all, shift, axis=0), 0.0)
        cg_last = cg[t_new - 1:t_new, :]
        e_all = jnp.exp(cg)
        w_all = jnp.exp(cg_last - cg)
        cg_t = pad_rows(cg).T
        for h in range(GDN_HEADS):
            p = (g, h)
            q8[p] = _l2norm(conv[:, h * dk:(h + 1) * dk]) * (dk ** -0.5)
            k8[p] = _l2norm(conv[:, GDN_DIM + h * dk:GDN_DIM + (h + 1) * dk])
            v8[p] = conv[:, 2 * GDN_DIM + h * dk:2 * GDN_DIM + (h + 1) * dk]
            e_col[p] = e_all[:, h:h + 1]
            b_col[p] = beta_all[:, GDN_HEADS + h:GDN_HEADS + h + 1]
            w_col[p] = w_all[:, h:h + 1]
            e_last[p] = e_all[t_new - 1:t_new, h:h + 1]
            decay[p] = jnp.exp(jnp.where(lane <= row, cg[:, h:h + 1] - cg_t[h:h + 1, :], -jnp.inf))

    states = {p: s_in_ref[p[0], p[1]] for p in problems}
    kq = {p: jnp.concatenate([k8[p], q8[p]], axis=0).astype(BF16) for p in problems}
    ks0_qs0 = {p: _dot(kq[p], states[p].astype(BF16)) for p in problems}
    gram = {p: _dot_nt(kq[p], pad_rows(k8[p]).astype(BF16)) for p in problems}
    deltas = {}
    for p in problems:
        m = b_col[p] * jnp.where(lane < row, gram[p][:SUBLANES] * decay[p], 0.0)
        r = b_col[p] * (v8[p] - e_col[p] * ks0_qs0[p][:SUBLANES])
        for j in range(t_new - 1):
            r = r - m[:, j:j + 1] * r[j:j + 1, :]
        deltas[p] = jnp.where(real, r, 0.0)
    updates = {}
    for p in problems:
        kw_t = pad_rows(jnp.where(real, k8[p] * w_col[p], 0.0)).T
        updates[p] = _dot(kw_t.astype(BF16), pad_rows(deltas[p]).astype(BF16))
    for g, h in problems:
        p = (g, h)
        cols = slice(h * dk, (h + 1) * dk)
        coef = gram[p][SUBLANES:] * decay[p]
        o = e_col[p] * ks0_qs0[p][SUBLANES:]
        for j in range(t_new):
            o = o + coef[:, j:j + 1] * deltas[p][j:j + 1, :]
        o = _gated_out_norm(o[:t_new], z_ref[g, :, cols], gw_ref[...])
        o_ref[g, :, cols] = o.astype(BF16)
        s_out_ref[g, h] = states[p] * e_last[p] + updates[p]


def _gdn_decode(u, hist, ab, z, s_in, conv_w, alog_pad, dtb_pad, gw):
    n_seq, t_new, _ = u.shape
    grp = DEC_GROUP
    seq = lambda r, width: pl.BlockSpec((grp, r, width), lambda i: (i, 0, 0))
    state = pl.BlockSpec((grp, GDN_HEADS, GDN_HEAD_DIM, GDN_HEAD_DIM), lambda i: (i, 0, 0, 0))
    return pl.pallas_call(
        functools.partial(_gdn_decode_kernel, t_new),
        grid=(n_seq // grp,),
        in_specs=[seq(t_new, CONV_DIM), seq(CONV_WIDTH - 1, CONV_DIM), seq(t_new, LANES), seq(t_new, GDN_DIM),
                  state, _const_spec((CONV_WIDTH, CONV_DIM)), _const_spec((1, LANES)), _const_spec((1, LANES)),
                  _const_spec((1, GDN_HEAD_DIM))],
        out_specs=(seq(t_new, GDN_DIM), state),
        out_shape=(jax.ShapeDtypeStruct((n_seq, t_new, GDN_DIM), BF16),
                   jax.ShapeDtypeStruct(s_in.shape, F32)),
        scratch_shapes=[pltpu.VMEM((grp, 2 * SUBLANES, CONV_DIM), F32), pltpu.VMEM((grp, SUBLANES, LANES), F32)],
        compiler_params=_params("parallel"),
        name="gdn_decode",
    )(u, hist, ab, z, s_in, conv_w, alog_pad, dtb_pad, gw)


def _merge_kernel(x_ref, oa_ref, ob_ref, sg_ref, wpa_ref, wpb_ref, wo_ref, nw_ref, x1_ref, hm_ref):
    y_a = _dot(oa_ref[...], wpa_ref[...])
    y_b = _dot(ob_ref[...], wpb_ref[...])
    mix = sg_ref[:, :D_MODEL] * y_a + sg_ref[:, D_MODEL:] * y_b
    x1 = x_ref[...] + _dot(mix.astype(BF16), wo_ref[...])
    x1_ref[...] = x1
    hm_ref[...] = _rmsnorm(x1, nw_ref[...]).astype(BF16)


def _merge(x2d, o_a, o_b, sg, w_pa, w_pb, w_o, norm_w, tm):
    n = x2d.shape[0]
    row = lambda width: pl.BlockSpec((tm, width), lambda i: (i, 0))
    return pl.pallas_call(
        _merge_kernel,
        grid=(n // tm,),
        in_specs=[row(D_MODEL), row(SB_DIM), row(GDN_DIM), row(2 * D_MODEL),
                  _const_spec(w_pa.shape), _const_spec(w_pb.shape), _const_spec(w_o.shape),
                  _const_spec((1, D_MODEL))],
        out_specs=(row(D_MODEL), row(D_MODEL)),
        out_shape=(jax.ShapeDtypeStruct((n, D_MODEL), F32), jax.ShapeDtypeStruct((n, D_MODEL), BF16)),
        compiler_params=_params("parallel"),
        name="merge",
    )(x2d, o_a, o_b, sg, w_pa, w_pb, w_o, norm_w)


def _mlp_kernel(x1_ref, hm_ref, wup_ref, wdown_ref, nf_ref, y_ref, acc_ref):
    f = pl.program_id(1)

    @pl.when(f == 0)
    def _():
        acc_ref[...] = x1_ref[...]

    up = jnp.maximum(_dot(hm_ref[...], wup_ref[...]), 0.0)
    acc_ref[...] += _dot((up * up).astype(BF16), wdown_ref[...])

    @pl.when(f == pl.num_programs(1) - 1)
    def _():
        y_ref[...] = _rmsnorm(acc_ref[...], nf_ref[...])


def _mlp(x1, hm, w_up, w_down, norm_f, tm, tf):
    n = x1.shape[0]
    return pl.pallas_call(
        _mlp_kernel,
        grid=(n // tm, D_FF // tf),
        in_specs=[pl.BlockSpec((tm, D_MODEL), lambda i, f: (i, 0)),
                  pl.BlockSpec((tm, D_MODEL), lambda i, f: (i, 0)),
                  pl.BlockSpec((D_MODEL, tf), lambda i, f: (0, f)),
                  pl.BlockSpec((tf, D_MODEL), lambda i, f: (f, 0)),
                  _const_spec((1, D_MODEL))],
        out_specs=pl.BlockSpec((tm, D_MODEL), lambda i, f: (i, 0)),
        out_shape=jax.ShapeDtypeStruct((n, D_MODEL), F32),
        scratch_shapes=[pltpu.VMEM((tm, D_MODEL), F32)],
        compiler_params=_params("parallel", "arbitrary"),
        name="mlp",
    )(x1, hm, w_up, w_down, norm_f)


def _permute_w_in(w):
    n_main = 3 * SB_DIM + CONV_DIM + GDN_DIM
    gates = w[:, n_main + 2 * GDN_HEADS:]
    ab = w[:, n_main:n_main + 2 * GDN_HEADS]
    pad = jnp.zeros((w.shape[0], LANES - 2 * GDN_HEADS), w.dtype)
    return jnp.concatenate([w[:, :n_main], gates, ab, pad], axis=1).astype(BF16)


def _lane_pad(vec, offset):
    return jnp.zeros((1, LANES), F32).at[0, offset:offset + vec.shape[0]].set(vec.astype(F32))


def _layer_common(x2d, lw, tm):
    return _inproj(x2d, lw["norm_mix_w"], lw["w_in"], tm)


def _layer_tail(x2d, o_a, o_b, sg, lw, norm_f, tm, tm_mlp, tf):
    x1, hm = _merge(x2d, o_a, o_b, sg, lw["w_pa"], lw["w_pb"], lw["w_o"], lw["norm_mlp_w"], tm)
    return _mlp(x1, hm, lw["w_up"], lw["w_down"], norm_f, tm_mlp, tf)


def kernel(x_prompt, x_sample, cache_k, cache_v, page_table, state_conv, state_ssm, norm_mix_w, w_in, sb_bias,
           conv_w, a_log, dt_bias, gdn_norm_w, w_pa, w_pb, w_o, norm_mlp_w, w_up, w_down, norm_final_w):
    depth = w_in.shape[0]
    assert depth == 1, "the residual stream is normalised once, after the only layer"
    b_p, t_p, _ = x_prompt.shape
    b_s, t_s, _ = x_sample.shape
    n_pool, page = cache_k.shape[1], cache_k.shape[2]
    l = 0
    lw = {
        "norm_mix_w": norm_mix_w[l].reshape(1, D_MODEL).astype(F32),
        "w_in": _permute_w_in(w_in[l]),
        "w_pa": w_pa[l].astype(BF16), "w_pb": w_pb[l].astype(BF16), "w_o": w_o[l].astype(BF16),
        "norm_mlp_w": norm_mlp_w[l].reshape(1, D_MODEL).astype(F32),
        "w_up": w_up[l].astype(BF16), "w_down": w_down[l].astype(BF16),
    }
    norm_f = norm_final_w.reshape(1, D_MODEL).astype(F32)
    bias = sb_bias[l].astype(F32)
    cw = conv_w[l].astype(F32)
    alog_pad = _lane_pad(a_log[l], 0)
    dtb_pad = _lane_pad(dt_bias[l], 0)
    gw = gdn_norm_w[l].reshape(1, GDN_HEAD_DIM).astype(F32)
    ntri = -jnp.tril(jnp.ones((SB_BLOCK, SB_BLOCK), BF16), -1)
    lincl = jnp.tril(jnp.ones((GDN_CHUNK, GDN_CHUNK), BF16))

    xp = x_prompt.reshape(b_p * t_p, D_MODEL).astype(F32)
    q, k, v, kb, vb, u, z, ab, sg = _layer_common(xp, lw, 256)
    shp = lambda a: a.reshape(b_p, t_p, a.shape[-1])
    o_a = _sb_prompt(bias, shp(q), shp(kb), shp(vb), ntri)
    un, wn, qg, kg, att, cg = _gdn_prep(shp(u), shp(ab), cw, alog_pad, dtb_pad, lincl)
    o_b, ssm_p = _gdn_scan(un, wn, qg, kg, att, cg, shp(z), gw)
    y_p = _layer_tail(xp, o_a.reshape(-1, SB_DIM), o_b.reshape(-1, GDN_DIM), sg, lw, norm_f, 512, 1024, 512)
    y_prompt = y_p.reshape(b_p, t_p, D_MODEL).astype(x_prompt.dtype)
    page_shape = (depth, b_p, t_p // page, page, SB_HEADS, SB_HEAD_DIM)
    new_k_prompt = k.reshape(page_shape).astype(cache_k.dtype)
    new_v_prompt = v.reshape(page_shape).astype(cache_v.dtype)
    new_conv_prompt = shp(u)[None, :, t_p - (CONV_WIDTH - 1):, :].astype(state_conv.dtype)
    new_ssm_prompt = ssm_p[None].astype(state_ssm.dtype)

    xs = x_sample.reshape(b_s * t_s, D_MODEL).astype(F32)
    tm_s = min(256, b_s * t_s)
    q, k, v, _, _, u, z, ab, sg = _layer_common(xs, lw, tm_s)
    shs = lambda a: a.reshape(b_s, t_s, a.shape[-1])
    pad8 = lambda a: jnp.pad(shs(a), ((0, 0), (0, SUBLANES - t_s), (0, 0)))
    pages_t = lambda c: jnp.transpose(c[l], (0, 2, 3, 1)).reshape(n_pool, SB_DIM, page).astype(F32)
    o_a = _sb_decode(page_table, bias, shs(q), pad8(k), pad8(v), ntri, pages_t(cache_k), pages_t(cache_v))
    o_b, ssm_s = _gdn_decode(shs(u), state_conv[l].astype(F32), shs(ab), shs(z), state_ssm[l].astype(F32),
                             cw, alog_pad, dtb_pad, gw)
    y_s = _layer_tail(xs, o_a.reshape(-1, SB_DIM), o_b.reshape(-1, GDN_DIM), sg, lw, norm_f, tm_s, tm_s, 512)
    y_sample = y_s.reshape(b_s, t_s, D_MODEL).astype(x_sample.dtype)
    head_shape = (depth, b_s, t_s, SB_HEADS, SB_HEAD_DIM)
    new_k_sample = k.reshape(head_shape).astype(cache_k.dtype)
    new_v_sample = v.reshape(head_shape).astype(cache_v.dtype)
    new_conv_sample = jnp.concatenate([state_conv[l].astype(F32), shs(u)], axis=1)[None, :, t_s:, :].astype(state_conv.dtype)
    new_ssm_sample = ssm_s[None].astype(state_ssm.dtype)

    return (y_prompt, y_sample, new_k_prompt, new_v_prompt, new_k_sample, new_v_sample,
            new_conv_prompt, new_conv_sample, new_ssm_prompt, new_ssm_sample)
```

```python
import functools
import math

import jax
import jax.numpy as jnp
from jax import lax
from jax.experimental import pallas as pl
from jax.experimental.pallas import tpu as pltpu

F32 = jnp.float32
BF16 = jnp.bfloat16

D_MODEL = 1024
SB_HEADS = 8
SB_HEAD_DIM = 64
SB_DIM = SB_HEADS * SB_HEAD_DIM
GDN_HEADS = 4
GDN_HEAD_DIM = 128
GDN_DIM = GDN_HEADS * GDN_HEAD_DIM
CONV_WIDTH = 4
CONV_DIM = 3 * GDN_DIM
GDN_CHUNK = 64
D_FF = 4 * D_MODEL
NORM_EPS = 1e-6
L2_EPS = 1e-6
NEG_LOG2E = -1.4426950408889634

LANES = 128
SUBLANES = 8
VMEM_LIMIT_BYTES = 56 * 1024 * 1024

COL_Q = 0
COL_K = COL_Q + SB_DIM
COL_V = COL_K + SB_DIM
COL_U = COL_V + SB_DIM
COL_Z = COL_U + CONV_DIM
COL_GA = COL_Z + GDN_DIM
COL_GB = COL_GA + D_MODEL
COL_AB = COL_GB + D_MODEL
IN_COLS = COL_AB + LANES

SB_BLOCK = 256
SB_ROW_CHUNK = 64
PREP_ROWS = 512
DEC_GROUP = 8


def _dot(a, b):
    return jnp.dot(a, b, preferred_element_type=F32)


def _dot_nt(a, b):
    return lax.dot_general(a, b, (((1,), (1,)), ((), ())), preferred_element_type=F32)


def _dot_tn(a, b):
    return lax.dot_general(a, b, (((0,), (0,)), ((), ())), preferred_element_type=F32)


def _dot_exact01(m01, x):
    h = x.astype(BF16)
    r = x - h.astype(F32)
    m = r.astype(BF16)
    l = (r - m.astype(F32)).astype(BF16)
    return _dot(m01, h) + (_dot(m01, m) + _dot(m01, l))


def _softplus(z):
    return jnp.maximum(z, 0.0) + jnp.log(1.0 + jnp.exp2(jnp.abs(z) * NEG_LOG2E))


def _sigmoid(z):
    return 1.0 / (1.0 + jnp.exp(-z))


def _silu(z):
    return z * _sigmoid(z)


def _rmsnorm(x, w):
    return x * lax.rsqrt(jnp.mean(x * x, axis=-1, keepdims=True) + NORM_EPS) * w


def _const_spec(shape):
    nd = len(shape)
    return pl.BlockSpec(shape, lambda *_: (0,) * nd, pipeline_mode=pl.Buffered(1))


def _params(*sem):
    return pltpu.CompilerParams(dimension_semantics=sem, vmem_limit_bytes=VMEM_LIMIT_BYTES)


def _inproj_kernel(x_ref, nw_ref, w_ref, q_ref, k_ref, v_ref, kb_ref, vb_ref, u_ref, z_ref, ab_ref, sg_ref):
    x = x_ref[...]
    hb = _rmsnorm(x, nw_ref[...]).astype(BF16)

    def seg(lo, width):
        return _dot(hb, w_ref[:, lo:lo + width])

    q_ref[...] = (seg(COL_Q, SB_DIM) * (SB_HEAD_DIM ** -0.5)).astype(BF16)
    k = seg(COL_K, SB_DIM)
    k_ref[...] = k
    kb_ref[...] = k.astype(BF16)
    v = seg(COL_V, SB_DIM)
    v_ref[...] = v
    vb_ref[...] = v.astype(BF16)
    for j in range(CONV_DIM // SB_DIM):
        u_ref[:, j * SB_DIM:(j + 1) * SB_DIM] = seg(COL_U + j * SB_DIM, SB_DIM)
    z_ref[...] = seg(COL_Z, GDN_DIM)
    for j in range(2 * D_MODEL // SB_DIM):
        sg_ref[:, j * SB_DIM:(j + 1) * SB_DIM] = _sigmoid(seg(COL_GA + j * SB_DIM, SB_DIM))
    ab_ref[...] = seg(COL_AB, LANES)


def _inproj(x2d, norm_w, w_perm, tm):
    n = x2d.shape[0]
    row = lambda width: pl.BlockSpec((tm, width), lambda i: (i, 0))
    out_shapes = (
        jax.ShapeDtypeStruct((n, SB_DIM), BF16),
        jax.ShapeDtypeStruct((n, SB_DIM), F32),
        jax.ShapeDtypeStruct((n, SB_DIM), F32),
        jax.ShapeDtypeStruct((n, SB_DIM), BF16),
        jax.ShapeDtypeStruct((n, SB_DIM), BF16),
        jax.ShapeDtypeStruct((n, CONV_DIM), F32),
        jax.ShapeDtypeStruct((n, GDN_DIM), F32),
        jax.ShapeDtypeStruct((n, LANES), F32),
        jax.ShapeDtypeStruct((n, 2 * D_MODEL), F32),
    )
    return pl.pallas_call(
        _inproj_kernel,
        grid=(n // tm,),
        in_specs=[row(D_MODEL), _const_spec((1, D_MODEL)), _const_spec((D_MODEL, IN_COLS))],
        out_specs=tuple(row(s.shape[1]) for s in out_shapes),
        out_shape=out_shapes,
        compiler_params=_params("parallel"),
        name="inproj",
    )(x2d, norm_w, w_perm)


def _sb_tile(z, carry, ntri, mask):
    sp = _softplus(z)
    if mask is not None:
        sp = jnp.where(mask, sp, 0.0)
    suffix = _dot(sp.astype(BF16), ntri)
    w = jnp.exp(z - sp + suffix + carry)
    if mask is not None:
        w = jnp.where(mask, w, 0.0)
    return w.astype(BF16), carry - jnp.sum(sp, axis=1, keepdims=True)


def _sb_prompt_kernel(bias_ref, q_ref, k_ref, v_ref, ntri_ref, o_ref,
                      qs_scr, bias_scr, z_scr, beta_scr, decay_scr, run_scr, acc_scr):
    hp = pl.program_id(1)
    tb = SB_BLOCK
    n_q = q_ref.shape[1] // tb
    n_tiles = n_q * (n_q + 1) // 2
    lane = lax.broadcasted_iota(jnp.int32, (1, LANES), 1)
    head_lanes = [lane < SB_HEAD_DIM, lane >= SB_HEAD_DIM]
    zero = jnp.zeros((), BF16)
    rows = [slice(j * tb, (j + 1) * tb) for j in range(2)]

    for ref in (z_scr, beta_scr, decay_scr, run_scr, acc_scr):
        ref[...] = jnp.zeros_like(ref)
    row = lax.broadcasted_iota(jnp.int32, (tb, tb), 0)
    col = lax.broadcasted_iota(jnp.int32, (tb, tb), 1)
    for j in range(2):
        bias = bias_ref[2 * hp + j]
        bias_scr[0, j] = jnp.full((tb, tb), bias, F32)
        bias_scr[1, j] = jnp.where(col < row, bias, -1e30)
    for i in range(n_q):
        q = q_ref[0, i * tb:(i + 1) * tb, :]
        for j in range(2):
            qs_scr[i, rows[j], :] = jnp.where(head_lanes[j], q, zero)

    def key_block(ref, kb):
        return ref[0, pl.ds(pl.multiple_of(kb * tb, tb), tb), :]

    def step(s, tiles):
        (q_a, k_a), (q_b, k_b), (q_d, k_d) = tiles
        valid_d = jnp.logical_and(s >= 2, s - 2 < n_tiles)
        v = key_block(v_ref, k_d)
        pv = None
        for j in range(2):
            vm = jnp.where(jnp.logical_and(head_lanes[j], valid_d), v, zero)
            w = beta_scr[rows[j], :] * decay_scr[rows[j], :]
            term = _dot(w, vm)
            pv = term if pv is None else pv + term
        acc_rows = pl.ds(pl.multiple_of(q_d * tb, tb), tb)
        acc_scr[acc_rows, :] = acc_scr[acc_rows, :] + pv
        first_of_block = k_b == q_b
        offset = first_of_block.astype(jnp.int32)
        one = jnp.ones((), BF16)
        chunks = [slice(c * SB_ROW_CHUNK, (c + 1) * SB_ROW_CHUNK) for c in range(2 * tb // SB_ROW_CHUNK)]
        sps = []
        for r in chunks:
            j, r_in_head = divmod(r.start, tb)
            z = z_scr[r, :] + bias_scr[offset, j, r_in_head:r_in_head + SB_ROW_CHUNK, :]
            t = jnp.exp2(jnp.abs(z) * NEG_LOG2E)
            log1p_t = jnp.log(1.0 + t).astype(BF16)
            z16, t16 = z.astype(BF16), t.astype(BF16)
            sps.append(jnp.maximum(z16, zero) + log1p_t)
            beta_scr[r, :] = jnp.where(z16 >= zero, one, t16) / (one + t16)
        suffix = _dot(jnp.concatenate(sps, axis=0), ntri_ref[...])
        for r, sp in zip(chunks, sps):
            run = jnp.where(first_of_block, 0.0, run_scr[r, :])
            decay_scr[r, :] = jnp.exp(suffix[r, :] + run).astype(BF16)
            run_scr[r, :] = run + suffix[r, 0:1] - sp[:, 0:1].astype(F32)
        z_scr[...] = _dot_nt(qs_scr[q_a], key_block(k_ref, k_a))
        wrap = k_a == 0
        q_n = jnp.where(wrap, jnp.minimum(q_a + 1, n_q - 1), q_a)
        k_n = jnp.where(wrap, q_n, k_a - 1)
        return ((q_n, k_n),) + tiles[:2]

    first = (jnp.int32(0), jnp.int32(0))
    lax.fori_loop(0, n_tiles + 2, step, (first,) * 3)
    o_ref[0] = acc_scr[...].astype(BF16)


def _sb_prompt(sb_bias, q, k, v, ntri):
    bsz, t_len, _ = q.shape
    tb = SB_BLOCK
    seq = pl.BlockSpec((1, t_len, LANES), lambda b, h, *_: (b, 0, h))
    return pl.pallas_call(
        _sb_prompt_kernel,
        grid_spec=pltpu.PrefetchScalarGridSpec(
            num_scalar_prefetch=1,
            grid=(bsz, SB_DIM // LANES),
            in_specs=[seq, seq, seq, pl.BlockSpec((tb, tb), lambda b, h, *_: (0, 0))],
            out_specs=seq,
            scratch_shapes=[
                pltpu.VMEM((t_len // tb, 2 * tb, LANES), BF16),
                pltpu.VMEM((2, 2, tb, tb), F32),
                pltpu.VMEM((2 * tb, tb), F32),
                pltpu.VMEM((2 * tb, tb), BF16),
                pltpu.VMEM((2 * tb, tb), BF16),
                pltpu.VMEM((2 * tb, 1), F32),
                pltpu.VMEM((t_len, LANES), F32),
            ],
        ),
        out_shape=jax.ShapeDtypeStruct((bsz, t_len, SB_DIM), BF16),
        compiler_params=_params("parallel", "parallel"),
        name="sb_prompt",
    )(sb_bias, q, k, v, ntri)


def _sb_decode_kernel(n_pages, t_new, pt_ref, bias_ref, q_ref, kn_ref, vn_ref, ntri_ref, *refs):
    kt_pages = refs[:n_pages]
    vt_pages = refs[n_pages:2 * n_pages]
    o_ref = refs[2 * n_pages]
    page = kt_pages[0].shape[2]
    rows = t_new * SB_HEADS
    r_iota = lax.broadcasted_iota(jnp.int32, (rows, SB_DIM), 0)
    l_iota = lax.broadcasted_iota(jnp.int32, (rows, SB_DIM), 1)
    head_of_row = r_iota % SB_HEADS
    own_lanes = (l_iota // SB_HEAD_DIM) == head_of_row

    q = q_ref[0].astype(F32)
    q_rep = jnp.broadcast_to(q[:, None, :], (t_new, SB_HEADS, SB_DIM)).reshape(rows, SB_DIM)
    q_bd = jnp.where(own_lanes, q_rep, 0.0).astype(BF16)

    r1 = lax.broadcasted_iota(jnp.int32, (rows, 1), 0)
    bias = jnp.zeros((rows, 1), F32)
    for h in range(SB_HEADS):
        bias = jnp.where(r1 % SB_HEADS == h, bias_ref[h], bias)

    pad = jnp.zeros((page - SUBLANES, SB_DIM), F32)
    k_new = jnp.concatenate([kn_ref[0], pad], axis=0).astype(BF16)
    v_new = jnp.concatenate([vn_ref[0], pad], axis=0).astype(BF16)
    kcol = lax.broadcasted_iota(jnp.int32, (rows, page), 1)
    qtok = lax.broadcasted_iota(jnp.int32, (rows, page), 0) // SB_HEADS
    z = _dot_nt(q_bd, k_new) + bias
    w, carry = _sb_tile(z, jnp.zeros((rows, 1), F32), ntri_ref[:page, :page], kcol < qtok)
    acc = _dot(w, v_new)

    blk = ntri_ref.shape[0] // page
    order = range(n_pages // blk - 1, -1, -1)
    lane_cat = lambda pages, pb: jnp.concatenate(
        [pages[pb * blk + i][0] for i in range(blk)], axis=1).astype(BF16)
    zs = [_dot(q_bd, lane_cat(kt_pages, pb)) + bias for pb in order]
    sps = [_softplus(z) for z in zs]
    suffixes = [_dot(sp.astype(BF16), ntri_ref[...]) for sp in sps]
    for pb, z, sp, suffix in zip(order, zs, sps, suffixes):
        w = jnp.exp(z - sp + suffix + carry).astype(BF16)
        carry = carry - jnp.sum(sp, axis=1, keepdims=True)
        acc = acc + _dot_nt(w, lane_cat(vt_pages, pb))

    acc = jnp.where(own_lanes, acc, 0.0)
    o_ref[0] = jnp.sum(acc.reshape(t_new, SB_HEADS, SB_DIM), axis=1).astype(BF16)


def _sb_decode(page_table, sb_bias, q, k_new8, v_new8, ntri, cache_kt, cache_vt):
    n_seq, t_new, _ = q.shape
    n_pages = page_table.shape[1]
    page = cache_kt.shape[2]

    def page_spec(p):
        return pl.BlockSpec((1, SB_DIM, page), lambda s, pt, b: (pt[s, p], 0, 0))

    seq_spec = lambda r: pl.BlockSpec((1, r, SB_DIM), lambda s, pt, b: (s, 0, 0))
    return pl.pallas_call(
        functools.partial(_sb_decode_kernel, n_pages, t_new),
        grid_spec=pltpu.PrefetchScalarGridSpec(
            num_scalar_prefetch=2,
            grid=(n_seq,),
            in_specs=[seq_spec(t_new), seq_spec(SUBLANES), seq_spec(SUBLANES),
                      pl.BlockSpec(ntri.shape, lambda s, pt, b: (0, 0))]
                     + [page_spec(p) for p in range(n_pages)] * 2,
            out_specs=seq_spec(t_new),
        ),
        out_shape=jax.ShapeDtypeStruct((n_seq, t_new, SB_DIM), BF16),
        compiler_params=_params("parallel"),
        name="sb_decode",
    )(page_table, sb_bias, q, k_new8, v_new8, ntri, *([cache_kt] * n_pages), *([cache_vt] * n_pages))


def _conv_silu(ext_ref, first, rows, cw_ref):
    acc = None
    for w in range(CONV_WIDTH):
        term = ext_ref[first + w:first + w + rows, :] * cw_ref[w:w + 1, :]
        acc = term if acc is None else acc + term
    return _silu(acc)


def _l2norm(x):
    return x * lax.rsqrt(jnp.sum(x * x, axis=-1, keepdims=True) + L2_EPS)


def _gate_terms(ab, alog_ref, dtb_ref):
    g = -jnp.exp(alog_ref[...]) * _softplus(ab + dtb_ref[...])
    return g, _sigmoid(ab)


def _gated_out_norm(o, z, gw):
    o = o * lax.rsqrt(jnp.mean(o * o, axis=-1, keepdims=True) + NORM_EPS) * gw
    return o * _silu(z)


def _gdn_prep_kernel(u_ref, prev_ref, ab_ref, cw_ref, alog_ref, dtb_ref, lincl_ref,
                     un_ref, wn_ref, qg_ref, kg_ref, att_ref, cg_ref, ext_ref):
    rows = u_ref.shape[1]
    c_len = GDN_CHUNK
    dk = GDN_HEAD_DIM
    first_block = pl.program_id(1) == 0
    prev = prev_ref[0]
    ext_ref[0:SUBLANES, :] = jnp.where(first_block, jnp.zeros_like(prev), prev)
    ext_ref[SUBLANES:, :] = u_ref[0]
    conv = _conv_silu(ext_ref, SUBLANES - (CONV_WIDTH - 1), rows, cw_ref)

    g_all, beta_all = _gate_terms(ab_ref[0], alog_ref, dtb_ref)
    r_i = lax.broadcasted_iota(jnp.int32, (c_len, c_len), 0)
    c_i = lax.broadcasted_iota(jnp.int32, (c_len, c_len), 1)
    incl = c_i <= r_i
    strict = c_i < r_i
    lincl = lincl_ref[...]
    n_chunks = rows // c_len

    cg_chunks = [_dot_exact01(lincl, g_all[c * c_len:(c + 1) * c_len, :]) for c in range(n_chunks)]
    cg_t_pairs = []
    for pair in range(n_chunks // 2):
        cg_pair = jnp.concatenate(cg_chunks[2 * pair:2 * pair + 2], axis=0)
        cg_ref[0, 2 * pair * c_len:(2 * pair + 2) * c_len, :] = cg_pair
        cg_t_pairs.append(cg_pair.T)

    problems = [(c, h) for c in range(n_chunks) for h in range(GDN_HEADS)]

    def head_cols(base, h):
        return slice(base + h * dk, base + (h + 1) * dk)

    qs, ks, vs, bcs, cgcols, decays = [], [], [], [], [], []
    for c, h in problems:
        r = slice(c * c_len, (c + 1) * c_len)
        qs.append(_l2norm(conv[r, head_cols(0, h)]) * (dk ** -0.5))
        ks.append(_l2norm(conv[r, head_cols(GDN_DIM, h)]))
        vs.append(conv[r, head_cols(2 * GDN_DIM, h)])
        bcs.append(beta_all[r, GDN_HEADS + h:GDN_HEADS + h + 1])
        cg_col = cg_chunks[c][:, h:h + 1]
        cg_row = cg_t_pairs[c // 2][h:h + 1, (c % 2) * c_len:(c % 2 + 1) * c_len]
        cgcols.append(cg_col)
        decays.append(jnp.exp(jnp.where(incl, cg_col - cg_row, -jnp.inf)))
    kbs = [k.astype(BF16) for k in ks]
    kks = [_dot_nt(kb, kb) for kb in kbs]

    powers = [jnp.where(strict, -(bc * kk * dec), 0.0) for bc, kk, dec in zip(bcs, kks, decays)]
    inv_off = powers
    for _ in range(int(math.log2(c_len)) - 1):
        pbs = [p.astype(BF16) for p in powers]
        powers = [_dot(pb, pb) for pb in pbs]
        next_pbs = [p.astype(BF16) for p in powers]
        inv_off = [t + p + _dot(t.astype(BF16), pb) for t, p, pb in zip(inv_off, powers, next_pbs)]

    for (c, h), q, k, v, bc, cg_col, dec, t_off in zip(problems, qs, ks, vs, bcs, cgcols, decays, inv_off):
        r = slice(c * c_len, (c + 1) * c_len)
        cols = head_cols(0, h)
        rhs = jnp.concatenate([v * bc, k * (bc * jnp.exp(cg_col))], axis=1)
        sol = rhs + _dot(t_off.astype(BF16), rhs.astype(BF16))
        un_ref[0, r, cols] = sol[:, :dk]
        wn_ref[0, r, cols] = sol[:, dk:].astype(BF16)
        qg_ref[0, r, cols] = (q * jnp.exp(cg_col)).astype(BF16)
        g_last = cg_chunks[c][c_len - 1:c_len, h:h + 1]
        kg_ref[0, r, cols] = (k * jnp.exp(g_last - cg_col)).astype(BF16)
        qk = _dot_nt(q.astype(BF16), k.astype(BF16))
        att_ref[0, h, r, :] = (qk * dec).astype(BF16)


def _gdn_prep(u, ab, conv_w, alog_pad, dtb_pad, lincl):
    bsz, t_len, _ = u.shape
    rows = min(PREP_ROWS, t_len)
    nb = rows // SUBLANES
    blk = lambda width: pl.BlockSpec((1, rows, width), lambda b, i: (b, i, 0))
    out_shapes = (
        jax.ShapeDtypeStruct((bsz, t_len, GDN_DIM), F32),
        jax.ShapeDtypeStruct((bsz, t_len, GDN_DIM), BF16),
        jax.ShapeDtypeStruct((bsz, t_len, GDN_DIM), BF16),
        jax.ShapeDtypeStruct((bsz, t_len, GDN_DIM), BF16),
        jax.ShapeDtypeStruct((bsz, GDN_HEADS, t_len, GDN_CHUNK), BF16),
        jax.ShapeDtypeStruct((bsz, t_len, LANES), F32),
    )
    return pl.pallas_call(
        _gdn_prep_kernel,
        grid=(bsz, t_len // rows),
        in_specs=[
            blk(CONV_DIM),
            pl.BlockSpec((1, SUBLANES, CONV_DIM), lambda b, i: (b, jnp.maximum(i * nb - 1, 0), 0)),
            blk(LANES),
            _const_spec((CONV_WIDTH, CONV_DIM)),
            _const_spec((1, LANES)),
            _const_spec((1, LANES)),
            _const_spec((GDN_CHUNK, GDN_CHUNK)),
        ],
        out_specs=(blk(GDN_DIM), blk(GDN_DIM), blk(GDN_DIM), blk(GDN_DIM),
                   pl.BlockSpec((1, GDN_HEADS, rows, GDN_CHUNK), lambda b, i: (b, 0, i, 0)),
                   blk(LANES)),
        out_shape=out_shapes,
        scratch_shapes=[pltpu.VMEM((rows + SUBLANES, CONV_DIM), F32)],
        compiler_params=_params("parallel", "parallel"),
        name="gdn_prep",
    )(u, u, ab, conv_w, alog_pad, dtb_pad, lincl)


def _gdn_scan_kernel(un_ref, wn_ref, qg_ref, kg_ref, att_ref, cg_ref, z_ref, gw_ref, o_ref, s_out_ref, s_ref):
    c = pl.program_id(0)
    bsz = un_ref.shape[0]
    c_len = GDN_CHUNK
    dk = GDN_HEAD_DIM

    @pl.when(c == 0)
    def _():
        s_ref[...] = jnp.zeros_like(s_ref)

    chains = [(b, h, slice(h * dk, (h + 1) * dk)) for b in range(bsz) for h in range(GDN_HEADS)]
    decay_last = [jnp.exp(cg_ref[b, c_len - 1:c_len, :]) for b in range(bsz)]
    states = [s_ref[b, h] for b, h, _ in chains]
    sbs = [s.astype(BF16) for s in states]
    ws = [_dot(wn_ref[b, :, cols], sb) for (b, _, cols), sb in zip(chains, sbs)]
    qs = [_dot(qg_ref[b, :, cols], sb) for (b, _, cols), sb in zip(chains, sbs)]
    vbs = [(un_ref[b, :, cols] - w).astype(BF16) for (b, _, cols), w in zip(chains, ws)]
    for (b, h, cols), s, q_s, vb in zip(chains, states, qs, vbs):
        o = q_s + _dot(att_ref[b, h], vb)
        s_ref[b, h] = s * decay_last[b][:, h:h + 1] + _dot_tn(kg_ref[b, :, cols], vb)
        o_ref[b, :, cols] = _gated_out_norm(o, z_ref[b, :, cols], gw_ref[...]).astype(BF16)

    @pl.when(c == pl.num_programs(0) - 1)
    def _():
        s_out_ref[...] = s_ref[...]


def _gdn_scan(un, wn, qg, kg, att, cg, z, gw):
    bsz, t_len, _ = un.shape
    c_len = GDN_CHUNK
    blk = lambda width: pl.BlockSpec((bsz, c_len, width), lambda c: (0, c, 0))
    state_shape = (bsz, GDN_HEADS, GDN_HEAD_DIM, GDN_HEAD_DIM)
    return pl.pallas_call(
        _gdn_scan_kernel,
        grid=(t_len // c_len,),
        in_specs=[blk(GDN_DIM), blk(GDN_DIM), blk(GDN_DIM), blk(GDN_DIM),
                  pl.BlockSpec((bsz, GDN_HEADS, c_len, c_len), lambda c: (0, 0, c, 0)),
                  blk(LANES), blk(GDN_DIM), _const_spec((1, GDN_HEAD_DIM))],
        out_specs=(blk(GDN_DIM), pl.BlockSpec(state_shape, lambda c: (0, 0, 0, 0))),
        out_shape=(jax.ShapeDtypeStruct((bsz, t_len, GDN_DIM), BF16),
                   jax.ShapeDtypeStruct(state_shape, F32)),
        scratch_shapes=[pltpu.VMEM(state_shape, F32)],
        compiler_params=_params("arbitrary"),
        name="gdn_scan",
    )(un, wn, qg, kg, att, cg, z, gw)


def _gdn_decode_kernel(t_new, u_ref, hist_ref, ab_ref, z_ref, s_in_ref, cw_ref, alog_ref, dtb_ref, gw_ref,
                       o_ref, s_out_ref, ext_ref, ab_scr):
    dk = GDN_HEAD_DIM
    n_hist = CONV_WIDTH - 1
    n_seq = u_ref.shape[0]
    tile = (SUBLANES, LANES)
    tail = jnp.zeros((LANES - SUBLANES, LANES), F32)
    pad_rows = lambda x: jnp.concatenate([x, tail], axis=0)
    row = lax.broadcasted_iota(jnp.int32, tile, 0)
    lane = lax.broadcasted_iota(jnp.int32, tile, 1)
    real = row < t_new
    ext_ref[...] = jnp.zeros_like(ext_ref)
    ab_scr[...] = jnp.zeros_like(ab_scr)

    problems = [(g, h) for g in range(n_seq) for h in range(GDN_HEADS)]
    q8, k8, v8, e_col, b_col, w_col, decay, e_last = {}, {}, {}, {}, {}, {}, {}, {}
    for g in range(n_seq):
        ext_ref[g, 0:n_hist, :] = hist_ref[g]
        ext_ref[g, n_hist:n_hist + t_new, :] = u_ref[g]
        ab_scr[g, 0:t_new, :] = ab_ref[g]
        conv = _conv_silu(ext_ref.at[g], 0, SUBLANES, cw_ref)
        g_all, beta_all = _gate_terms(ab_scr[g], alog_ref, dtb_ref)
        cg = g_all
        for shift in range(1, t_new):
            cg = cg + jnp.where(row >= shift, pltpu.roll(g_all, shift, axis=0), 0.0)
        cg_last = cg[t_new - 1:t_new, :]
        e_all = jnp.exp(cg)
        w_all = jnp.exp(cg_last - cg)
        cg_t = pad_rows(cg).T
        for h in range(GDN_HEADS):
            p = (g, h)
            q8[p] = _l2norm(conv[:, h * dk:(h + 1) * dk]) * (dk ** -0.5)
            k8[p] = _l2norm(conv[:, GDN_DIM + h * dk:GDN_DIM + (h + 1) * dk])
            v8[p] = conv[:, 2 * GDN_DIM + h * dk:2 * GDN_DIM + (h + 1) * dk]
            e_col[p] = e_all[:, h:h + 1]
            b_col[p] = beta_all[:, GDN_HEADS + h:GDN_HEADS + h + 1]
            w_col[p] = w_all[:, h:h + 1]
            e_last[p] = e_all[t_new - 1:t_new, h:h + 1]
            decay[p] = jnp.exp(jnp.where(lane <= row, cg[:, h:h + 1] - cg_t[h:h + 1, :], -jnp.inf))

    states = {p: s_in_ref[p[0], p[1]] for p in problems}
    kq = {p: jnp.concatenate([k8[p], q8[p]], axis=0).astype(BF16) for p in problems}
    ks0_qs0 = {p: _dot(kq[p], states[p].astype(BF16)) for p in problems}
    gram = {p: _dot_nt(kq[p], pad_rows(k8[p]).astype(BF16)) for p in problems}
    deltas = {}
    for p in problems:
        m = b_col[p] * jnp.where(lane < row, gram[p][:SUBLANES] * decay[p], 0.0)
        r = b_col[p] * (v8[p] - e_col[p] * ks0_qs0[p][:SUBLANES])
        for j in range(t_new - 1):
            r = r - m[:, j:j + 1] * r[j:j + 1, :]
        deltas[p] = jnp.where(real, r, 0.0)
    updates = {}
    for p in problems:
        kw_t = pad_rows(jnp.where(real, k8[p] * w_col[p], 0.0)).T
        updates[p] = _dot(kw_t.astype(BF16), pad_rows(deltas[p]).astype(BF16))
    for g, h in problems:
        p = (g, h)
        cols = slice(h * dk, (h + 1) * dk)
        coef = gram[p][SUBLANES:] * decay[p]
        o = e_col[p] * ks0_qs0[p][SUBLANES:]
        for j in range(t_new):
            o = o + coef[:, j:j + 1] * deltas[p][j:j + 1, :]
        o = _gated_out_norm(o[:t_new], z_ref[g, :, cols], gw_ref[...])
        o_ref[g, :, cols] = o.astype(BF16)
        s_out_ref[g, h] = states[p] * e_last[p] + updates[p]


def _gdn_decode(u, hist, ab, z, s_in, conv_w, alog_pad, dtb_pad, gw):
    n_seq, t_new, _ = u.shape
    grp = DEC_GROUP
    seq = lambda r, width: pl.BlockSpec((grp, r, width), lambda i: (i, 0, 0))
    state = pl.BlockSpec((grp, GDN_HEADS, GDN_HEAD_DIM, GDN_HEAD_DIM), lambda i: (i, 0, 0, 0))
    return pl.pallas_call(
        functools.partial(_gdn_decode_kernel, t_new),
        grid=(n_seq // grp,),
        in_specs=[seq(t_new, CONV_DIM), seq(CONV_WIDTH - 1, CONV_DIM), seq(t_new, LANES), seq(t_new, GDN_DIM),
                  state, _const_spec((CONV_WIDTH, CONV_DIM)), _const_spec((1, LANES)), _const_spec((1, LANES)),
                  _const_spec((1, GDN_HEAD_DIM))],
        out_specs=(seq(t_new, GDN_DIM), state),
        out_shape=(jax.ShapeDtypeStruct((n_seq, t_new, GDN_DIM), BF16),
                   jax.ShapeDtypeStruct(s_in.shape, F32)),
        scratch_shapes=[pltpu.VMEM((grp, 2 * SUBLANES, CONV_DIM), F32), pltpu.VMEM((grp, SUBLANES, LANES), F32)],
        compiler_params=_params("parallel"),
        name="gdn_decode",
    )(u, hist, ab, z, s_in, conv_w, alog_pad, dtb_pad, gw)


def _merge_kernel(x_ref, oa_ref, ob_ref, sg_ref, wpa_ref, wpb_ref, wo_ref, nw_ref, x1_ref, hm_ref):
    y_a = _dot(oa_ref[...], wpa_ref[...])
    y_b = _dot(ob_ref[...], wpb_ref[...])
    mix = sg_ref[:, :D_MODEL] * y_a + sg_ref[:, D_MODEL:] * y_b
    x1 = x_ref[...] + _dot(mix.astype(BF16), wo_ref[...])
    x1_ref[...] = x1
    hm_ref[...] = _rmsnorm(x1, nw_ref[...]).astype(BF16)


def _merge(x2d, o_a, o_b, sg, w_pa, w_pb, w_o, norm_w, tm):
    n = x2d.shape[0]
    row = lambda width: pl.BlockSpec((tm, width), lambda i: (i, 0))
    return pl.pallas_call(
        _merge_kernel,
        grid=(n // tm,),
        in_specs=[row(D_MODEL), row(SB_DIM), row(GDN_DIM), row(2 * D_MODEL),
                  _const_spec(w_pa.shape), _const_spec(w_pb.shape), _const_spec(w_o.shape),
                  _const_spec((1, D_MODEL))],
        out_specs=(row(D_MODEL), row(D_MODEL)),
        out_shape=(jax.ShapeDtypeStruct((n, D_MODEL), F32), jax.ShapeDtypeStruct((n, D_MODEL), BF16)),
        compiler_params=_params("parallel"),
        name="merge",
    )(x2d, o_a, o_b, sg, w_pa, w_pb, w_o, norm_w)


def _mlp_kernel(x1_ref, hm_ref, wup_ref, wdown_ref, nf_ref, y_ref, acc_ref):
    f = pl.program_id(1)

    @pl.when(f == 0)
    def _():
        acc_ref[...] = x1_ref[...]

    up = jnp.maximum(_dot(hm_ref[...], wup_ref[...]), 0.0)
    acc_ref[...] += _dot((up * up).astype(BF16), wdown_ref[...])

    @pl.when(f == pl.num_programs(1) - 1)
    def _():
        y_ref[...] = _rmsnorm(acc_ref[...], nf_ref[...])


def _mlp(x1, hm, w_up, w_down, norm_f, tm, tf):
    n = x1.shape[0]
    return pl.pallas_call(
        _mlp_kernel,
        grid=(n // tm, D_FF // tf),
        in_specs=[pl.BlockSpec((tm, D_MODEL), lambda i, f: (i, 0)),
                  pl.BlockSpec((tm, D_MODEL), lambda i, f: (i, 0)),
                  pl.BlockSpec((D_MODEL, tf), lambda i, f: (0, f)),
                  pl.BlockSpec((tf, D_MODEL), lambda i, f: (f, 0)),
                  _const_spec((1, D_MODEL))],
        out_specs=pl.BlockSpec((tm, D_MODEL), lambda i, f: (i, 0)),
        out_shape=jax.ShapeDtypeStruct((n, D_MODEL), F32),
        scratch_shapes=[pltpu.VMEM((tm, D_MODEL), F32)],
        compiler_params=_params("parallel", "arbitrary"),
        name="mlp",
    )(x1, hm, w_up, w_down, norm_f)


def _permute_w_in(w):
    n_main = 3 * SB_DIM + CONV_DIM + GDN_DIM
    gates = w[:, n_main + 2 * GDN_HEADS:]
    ab = w[:, n_main:n_main + 2 * GDN_HEADS]
    pad = jnp.zeros((w.shape[0], LANES - 2 * GDN_HEADS), w.dtype)
    return jnp.concatenate([w[:, :n_main], gates, ab, pad], axis=1).astype(BF16)


def _lane_pad(vec, offset):
    return jnp.zeros((1, LANES), F32).at[0, offset:offset + vec.shape[0]].set(vec.astype(F32))


def _layer_common(x2d, lw, tm):
    return _inproj(x2d, lw["norm_mix_w"], lw["w_in"], tm)


def _layer_tail(x2d, o_a, o_b, sg, lw, norm_f, tm, tm_mlp, tf):
    x1, hm = _merge(x2d, o_a, o_b, sg, lw["w_pa"], lw["w_pb"], lw["w_o"], lw["norm_mlp_w"], tm)
    return _mlp(x1, hm, lw["w_up"], lw["w_down"], norm_f, tm_mlp, tf)


def kernel(x_prompt, x_sample, cache_k, cache_v, page_table, state_conv, state_ssm, norm_mix_w, w_in, sb_bias,
           conv_w, a_log, dt_bias, gdn_norm_w, w_pa, w_pb, w_o, norm_mlp_w, w_up, w_down, norm_final_w):
    depth = w_in.shape[0]
    assert depth == 1, "the residual stream is normalised once, after the only layer"
    b_p, t_p, _ = x_prompt.shape
    b_s, t_s, _ = x_sample.shape
    n_pool, page = cache_k.shape[1], cache_k.shape[2]
    l = 0
    lw = {
        "norm_mix_w": norm_mix_w[l].reshape(1, D_MODEL).astype(F32),
        "w_in": _permute_w_in(w_in[l]),
        "w_pa": w_pa[l].astype(BF16), "w_pb": w_pb[l].astype(BF16), "w_o": w_o[l].astype(BF16),
        "norm_mlp_w": norm_mlp_w[l].reshape(1, D_MODEL).astype(F32),
        "w_up": w_up[l].astype(BF16), "w_down": w_down[l].astype(BF16),
    }
    norm_f = norm_final_w.reshape(1, D_MODEL).astype(F32)
    bias = sb_bias[l].astype(F32)
    cw = conv_w[l].astype(F32)
    alog_pad = _lane_pad(a_log[l], 0)
    dtb_pad = _lane_pad(dt_bias[l], 0)
    gw = gdn_norm_w[l].reshape(1, GDN_HEAD_DIM).astype(F32)
    ntri = -jnp.tril(jnp.ones((SB_BLOCK, SB_BLOCK), BF16), -1)
    lincl = jnp.tril(jnp.ones((GDN_CHUNK, GDN_CHUNK), BF16))

    xp = x_prompt.reshape(b_p * t_p, D_MODEL).astype(F32)
    q, k, v, kb, vb, u, z, ab, sg = _layer_common(xp, lw, 256)
    shp = lambda a: a.reshape(b_p, t_p, a.shape[-1])
    o_a = _sb_prompt(bias, shp(q), shp(kb), shp(vb), ntri)
    un, wn, qg, kg, att, cg = _gdn_prep(shp(u), shp(ab), cw, alog_pad, dtb_pad, lincl)
    o_b, ssm_p = _gdn_scan(un, wn, qg, kg, att, cg, shp(z), gw)
    y_p = _layer_tail(xp, o_a.reshape(-1, SB_DIM), o_b.reshape(-1, GDN_DIM), sg, lw, norm_f, 512, 1024, 512)
    y_prompt = y_p.reshape(b_p, t_p, D_MODEL).astype(x_prompt.dtype)
    page_shape = (depth, b_p, t_p // page, page, SB_HEADS, SB_HEAD_DIM)
    new_k_prompt = k.reshape(page_shape).astype(cache_k.dtype)
    new_v_prompt = v.reshape(page_shape).astype(cache_v.dtype)
    new_conv_prompt = shp(u)[None, :, t_p - (CONV_WIDTH - 1):, :].astype(state_conv.dtype)
    new_ssm_prompt = ssm_p[None].astype(state_ssm.dtype)

    xs = x_sample.reshape(b_s * t_s, D_MODEL).astype(F32)
    tm_s = min(256, b_s * t_s)
    q, k, v, _, _, u, z, ab, sg = _layer_common(xs, lw, tm_s)
    shs = lambda a: a.reshape(b_s, t_s, a.shape[-1])
    pad8 = lambda a: jnp.pad(shs(a), ((0, 0), (0, SUBLANES - t_s), (0, 0)))
    pages_t = lambda c: jnp.transpose(c[l], (0, 2, 3, 1)).reshape(n_pool, SB_DIM, page).astype(F32)
    o_a = _sb_decode(page_table, bias, shs(q), pad8(k), pad8(v), ntri, pages_t(cache_k), pages_t(cache_v))
    o_b, ssm_s = _gdn_decode(shs(u), state_conv[l].astype(F32), shs(ab), shs(z), state_ssm[l].astype(F32),
                             cw, alog_pad, dtb_pad, gw)
    y_s = _layer_tail(xs, o_a.reshape(-1, SB_DIM), o_b.reshape(-1, GDN_DIM), sg, lw, norm_f, tm_s, tm_s, 512)
    y_sample = y_s.reshape(b_s, t_s, D_MODEL).astype(x_sample.dtype)
    head_shape = (depth, b_s, t_s, SB_HEADS, SB_HEAD_DIM)
    new_k_sample = k.reshape(head_shape).astype(cache_k.dtype)
    new_v_sample = v.reshape(head_shape).astype(cache_v.dtype)
    new_conv_sample = jnp.concatenate([state_conv[l].astype(F32), shs(u)], axis=1)[None, :, t_s:, :].astype(state_conv.dtype)
    new_ssm_sample = ssm_s[None].astype(state_ssm.dtype)

    return (y_prompt, y_sample, new_k_prompt, new_v_prompt, new_k_sample, new_v_sample,
            new_conv_prompt, new_conv_sample, new_ssm_prompt, new_ssm_sample)
```

```python
import functools
import math

import jax
import jax.numpy as jnp
from jax import lax
from jax.experimental import pallas as pl
from jax.experimental.pallas import tpu as pltpu

F32 = jnp.float32
BF16 = jnp.bfloat16

D_MODEL = 1024
SB_HEADS = 8
SB_HEAD_DIM = 64
SB_DIM = SB_HEADS * SB_HEAD_DIM
GDN_HEADS = 4
GDN_HEAD_DIM = 128
GDN_DIM = GDN_HEADS * GDN_HEAD_DIM
CONV_WIDTH = 4
CONV_DIM = 3 * GDN_DIM
GDN_CHUNK = 64
D_FF = 4 * D_MODEL
NORM_EPS = 1e-6
L2_EPS = 1e-6
NEG_LOG2E = -1.4426950408889634

LANES = 128
SUBLANES = 8
VMEM_LIMIT_BYTES = 56 * 1024 * 1024

COL_Q = 0
COL_K = COL_Q + SB_DIM
COL_V = COL_K + SB_DIM
COL_U = COL_V + SB_DIM
COL_Z = COL_U + CONV_DIM
COL_GA = COL_Z + GDN_DIM
COL_GB = COL_GA + D_MODEL
COL_AB = COL_GB + D_MODEL
IN_COLS = COL_AB + LANES

SB_BLOCK = 256
SB_ROW_CHUNK = 64
PREP_ROWS = 512
SCAN_ROWS = 128
DEC_GROUP = 8


def _dot(a, b):
    return jnp.dot(a, b, preferred_element_type=F32)


def _dot_nt(a, b):
    return lax.dot_general(a, b, (((1,), (1,)), ((), ())), preferred_element_type=F32)


def _dot_tn(a, b):
    return lax.dot_general(a, b, (((0,), (0,)), ((), ())), preferred_element_type=F32)


def _dot_exact01(m01, x):
    h = x.astype(BF16)
    r = x - h.astype(F32)
    m = r.astype(BF16)
    l = (r - m.astype(F32)).astype(BF16)
    return _dot(m01, h) + (_dot(m01, m) + _dot(m01, l))


def _softplus(z):
    return jnp.maximum(z, 0.0) + jnp.log(1.0 + jnp.exp2(jnp.abs(z) * NEG_LOG2E))


def _sigmoid(z):
    return 1.0 / (1.0 + jnp.exp(-z))


def _silu(z):
    return z * _sigmoid(z)


def _rmsnorm(x, w):
    return x * lax.rsqrt(jnp.mean(x * x, axis=-1, keepdims=True) + NORM_EPS) * w


def _const_spec(shape):
    nd = len(shape)
    return pl.BlockSpec(shape, lambda *_: (0,) * nd, pipeline_mode=pl.Buffered(1))


def _params(*sem):
    return pltpu.CompilerParams(dimension_semantics=sem, vmem_limit_bytes=VMEM_LIMIT_BYTES)


def _inproj_kernel(page, x_ref, nw_ref, w_ref, q_ref, k_ref, v_ref, kb_ref, vb_ref, u_ref, z_ref, ab_ref, sg_ref):
    x = x_ref[...]
    hb = _rmsnorm(x, nw_ref[...]).astype(BF16)

    def seg(lo, width):
        return _dot(hb, w_ref[:, lo:lo + width])

    def store_kv(ref, val):
        if page is None:
            ref[...] = val
        else:
            val_t = val.T
            for p in range(val.shape[0] // page):
                ref[p] = val_t[:, p * page:(p + 1) * page]

    q_ref[...] = (seg(COL_Q, SB_DIM) * (SB_HEAD_DIM ** -0.5)).astype(BF16)
    k = seg(COL_K, SB_DIM)
    store_kv(k_ref, k)
    kb_ref[...] = k.astype(BF16)
    v = seg(COL_V, SB_DIM)
    store_kv(v_ref, v)
    vb_ref[...] = v.astype(BF16)
    for j in range(CONV_DIM // SB_DIM):
        u_ref[:, j * SB_DIM:(j + 1) * SB_DIM] = seg(COL_U + j * SB_DIM, SB_DIM)
    z_ref[...] = seg(COL_Z, GDN_DIM).astype(BF16)
    for j in range(2 * D_MODEL // SB_DIM):
        sg_ref[:, j * SB_DIM:(j + 1) * SB_DIM] = _sigmoid(seg(COL_GA + j * SB_DIM, SB_DIM)).astype(BF16)
    ab_ref[...] = seg(COL_AB, LANES)


def _inproj(x2d, norm_w, w_perm, tm, page=None):
    n = x2d.shape[0]
    row = lambda width: pl.BlockSpec((tm, width), lambda i: (i, 0))
    if page is None:
        kv_shape, kv_spec = jax.ShapeDtypeStruct((n, SB_DIM), F32), row(SB_DIM)
    else:
        kv_shape = jax.ShapeDtypeStruct((n // page, SB_DIM, page), F32)
        kv_spec = pl.BlockSpec((tm // page, SB_DIM, page), lambda i: (i, 0, 0))
    out_shapes = (
        jax.ShapeDtypeStruct((n, SB_DIM), BF16),
        kv_shape,
        kv_shape,
        jax.ShapeDtypeStruct((n, SB_DIM), BF16),
        jax.ShapeDtypeStruct((n, SB_DIM), BF16),
        jax.ShapeDtypeStruct((n, CONV_DIM), F32),
        jax.ShapeDtypeStruct((n, GDN_DIM), BF16),
        jax.ShapeDtypeStruct((n, LANES), F32),
        jax.ShapeDtypeStruct((n, 2 * D_MODEL), BF16),
    )
    out_specs = tuple(kv_spec if i in (1, 2) else row(s.shape[1]) for i, s in enumerate(out_shapes))
    return pl.pallas_call(
        functools.partial(_inproj_kernel, page),
        grid=(n // tm,),
        in_specs=[row(D_MODEL), _const_spec((1, D_MODEL)), _const_spec((D_MODEL, IN_COLS))],
        out_specs=out_specs,
        out_shape=out_shapes,
        compiler_params=_params("parallel"),
        name="inproj",
    )(x2d, norm_w, w_perm)


def _sb_tile(z, carry, ntri, mask):
    sp = _softplus(z)
    if mask is not None:
        sp = jnp.where(mask, sp, 0.0)
    suffix = _dot(sp.astype(BF16), ntri)
    w = jnp.exp(z - sp + suffix + carry)
    if mask is not None:
        w = jnp.where(mask, w, 0.0)
    return w.astype(BF16), carry - jnp.sum(sp, axis=1, keepdims=True)


def _sb_prompt_kernel(bias_ref, q_ref, k_ref, v_ref, ntri_ref, o_ref,
                      qs_scr, bias_scr, z_scr, beta_scr, decay_scr, run_scr, acc_scr):
    hp = pl.program_id(1)
    tb = SB_BLOCK
    n_q = q_ref.shape[1] // tb
    n_tiles = n_q * (n_q + 1) // 2
    lane = lax.broadcasted_iota(jnp.int32, (1, LANES), 1)
    head_lanes = [lane < SB_HEAD_DIM, lane >= SB_HEAD_DIM]
    zero = jnp.zeros((), BF16)
    rows = [slice(j * tb, (j + 1) * tb) for j in range(2)]

    for ref in (z_scr, beta_scr, decay_scr, run_scr, acc_scr):
        ref[...] = jnp.zeros_like(ref)
    row = lax.broadcasted_iota(jnp.int32, (tb, tb), 0)
    col = lax.broadcasted_iota(jnp.int32, (tb, tb), 1)
    for j in range(2):
        bias = bias_ref[2 * hp + j]
        bias_scr[0, j] = jnp.full((tb, tb), bias, F32)
        bias_scr[1, j] = jnp.where(col < row, bias, -1e30)
    for i in range(n_q):
        q = q_ref[0, i * tb:(i + 1) * tb, :]
        for j in range(2):
            qs_scr[i, rows[j], :] = jnp.where(head_lanes[j], q, zero)

    def key_block(ref, kb):
        return ref[0, pl.ds(pl.multiple_of(kb * tb, tb), tb), :]

    def step(s, tiles):
        (q_a, k_a), (q_b, k_b), (q_d, k_d) = tiles
        valid_d = jnp.logical_and(s >= 2, s - 2 < n_tiles)
        v = key_block(v_ref, k_d)
        pv = None
        for j in range(2):
            vm = jnp.where(jnp.logical_and(head_lanes[j], valid_d), v, zero)
            w = beta_scr[rows[j], :] * decay_scr[rows[j], :]
            term = _dot(w, vm)
            pv = term if pv is None else pv + term
        acc_rows = pl.ds(pl.multiple_of(q_d * tb, tb), tb)
        acc_scr[acc_rows, :] = acc_scr[acc_rows, :] + pv
        first_of_block = k_b == q_b
        offset = first_of_block.astype(jnp.int32)
        one = jnp.ones((), BF16)
        chunks = [slice(c * SB_ROW_CHUNK, (c + 1) * SB_ROW_CHUNK) for c in range(2 * tb // SB_ROW_CHUNK)]
        sps = []
        for r in chunks:
            j, r_in_head = divmod(r.start, tb)
            z = z_scr[r, :] + bias_scr[offset, j, r_in_head:r_in_head + SB_ROW_CHUNK, :]
            t = jnp.exp2(jnp.abs(z) * NEG_LOG2E)
            log1p_t = jnp.log(1.0 + t).astype(BF16)
            z16, t16 = z.astype(BF16), t.astype(BF16)
            sps.append(jnp.maximum(z16, zero) + log1p_t)
            beta_scr[r, :] = jnp.where(z16 >= zero, one, t16) / (one + t16)
        suffix = _dot(jnp.concatenate(sps, axis=0), ntri_ref[...])
        for r, sp in zip(chunks, sps):
            run = jnp.where(first_of_block, 0.0, run_scr[r, :])
            decay_scr[r, :] = jnp.exp(suffix[r, :] + run).astype(BF16)
            run_scr[r, :] = run + suffix[r, 0:1] - sp[:, 0:1].astype(F32)
        z_scr[...] = _dot_nt(qs_scr[q_a], key_block(k_ref, k_a))
        wrap = k_a == 0
        q_n = jnp.where(wrap, jnp.minimum(q_a + 1, n_q - 1), q_a)
        k_n = jnp.where(wrap, q_n, k_a - 1)
        return ((q_n, k_n),) + tiles[:2]

    first = (jnp.int32(0), jnp.int32(0))
    lax.fori_loop(0, n_tiles + 2, step, (first,) * 3)
    o_ref[0] = acc_scr[...].astype(BF16)


def _sb_prompt(sb_bias, q, k, v, ntri):
    bsz, t_len, _ = q.shape
    tb = SB_BLOCK
    seq = pl.BlockSpec((1, t_len, LANES), lambda b, h, *_: (b, 0, h))
    return pl.pallas_call(
        _sb_prompt_kernel,
        grid_spec=pltpu.PrefetchScalarGridSpec(
            num_scalar_prefetch=1,
            grid=(bsz, SB_DIM // LANES),
            in_specs=[seq, seq, seq, pl.BlockSpec((tb, tb), lambda b, h, *_: (0, 0))],
            out_specs=seq,
            scratch_shapes=[
                pltpu.VMEM((t_len // tb, 2 * tb, LANES), BF16),
                pltpu.VMEM((2, 2, tb, tb), F32),
                pltpu.VMEM((2 * tb, tb), F32),
                pltpu.VMEM((2 * tb, tb), BF16),
                pltpu.VMEM((2 * tb, tb), BF16),
                pltpu.VMEM((2 * tb, 1), F32),
                pltpu.VMEM((t_len, LANES), F32),
            ],
        ),
        out_shape=jax.ShapeDtypeStruct((bsz, t_len, SB_DIM), BF16),
        compiler_params=_params("parallel", "parallel"),
        name="sb_prompt",
    )(sb_bias, q, k, v, ntri)


def _sb_decode_kernel(n_pages, t_new, pt_ref, bias_ref, q_ref, kn_ref, vn_ref, ntri_ref, *refs):
    kt_pages = refs[:n_pages]
    vt_pages = refs[n_pages:2 * n_pages]
    o_ref = refs[2 * n_pages]
    page = kt_pages[0].shape[2]
    rows = t_new * SB_HEADS
    r_iota = lax.broadcasted_iota(jnp.int32, (rows, SB_DIM), 0)
    l_iota = lax.broadcasted_iota(jnp.int32, (rows, SB_DIM), 1)
    head_of_row = r_iota % SB_HEADS
    own_lanes = (l_iota // SB_HEAD_DIM) == head_of_row

    q = q_ref[0].astype(F32)
    q_rep = jnp.broadcast_to(q[:, None, :], (t_new, SB_HEADS, SB_DIM)).reshape(rows, SB_DIM)
    q_bd = jnp.where(own_lanes, q_rep, 0.0).astype(BF16)

    r1 = lax.broadcasted_iota(jnp.int32, (rows, 1), 0)
    bias = jnp.zeros((rows, 1), F32)
    for h in range(SB_HEADS):
        bias = jnp.where(r1 % SB_HEADS == h, bias_ref[h], bias)

    pad = jnp.zeros((page - SUBLANES, SB_DIM), F32)
    k_new = jnp.concatenate([kn_ref[0], pad], axis=0).astype(BF16)
    v_new = jnp.concatenate([vn_ref[0], pad], axis=0).astype(BF16)
    kcol = lax.broadcasted_iota(jnp.int32, (rows, page), 1)
    qtok = lax.broadcasted_iota(jnp.int32, (rows, page), 0) // SB_HEADS
    z = _dot_nt(q_bd, k_new) + bias
    w, carry = _sb_tile(z, jnp.zeros((rows, 1), F32), ntri_ref[:page, :page], kcol < qtok)
    acc = _dot(w, v_new)

    blk = ntri_ref.shape[0] // page
    order = range(n_pages // blk - 1, -1, -1)
    lane_cat = lambda pages, pb: jnp.concatenate(
        [pages[pb * blk + i][0] for i in range(blk)], axis=1).astype(BF16)
    zs = [_dot(q_bd, lane_cat(kt_pages, pb)) + bias for pb in order]
    sps = [_softplus(z) for z in zs]
    suffixes = [_dot(sp.astype(BF16), ntri_ref[...]) for sp in sps]
    for pb, z, sp, suffix in zip(order, zs, sps, suffixes):
        w = jnp.exp(z - sp + suffix + carry).astype(BF16)
        carry = carry - jnp.sum(sp, axis=1, keepdims=True)
        acc = acc + _dot_nt(w, lane_cat(vt_pages, pb))

    acc = jnp.where(own_lanes, acc, 0.0)
    o_ref[0] = jnp.sum(acc.reshape(t_new, SB_HEADS, SB_DIM), axis=1).astype(BF16)


def _sb_decode(page_table, sb_bias, q, k_new8, v_new8, ntri, cache_kt, cache_vt):
    n_seq, t_new, _ = q.shape
    n_pages = page_table.shape[1]
    page = cache_kt.shape[2]

    def page_spec(p):
        return pl.BlockSpec((1, SB_DIM, page), lambda s, pt, b: (pt[s, p], 0, 0))

    seq_spec = lambda r: pl.BlockSpec((1, r, SB_DIM), lambda s, pt, b: (s, 0, 0))
    return pl.pallas_call(
        functools.partial(_sb_decode_kernel, n_pages, t_new),
        grid_spec=pltpu.PrefetchScalarGridSpec(
            num_scalar_prefetch=2,
            grid=(n_seq,),
            in_specs=[seq_spec(t_new), seq_spec(SUBLANES), seq_spec(SUBLANES),
                      pl.BlockSpec(ntri.shape, lambda s, pt, b: (0, 0))]
                     + [page_spec(p) for p in range(n_pages)] * 2,
            out_specs=seq_spec(t_new),
        ),
        out_shape=jax.ShapeDtypeStruct((n_seq, t_new, SB_DIM), BF16),
        compiler_params=_params("parallel"),
        name="sb_decode",
    )(page_table, sb_bias, q, k_new8, v_new8, ntri, *([cache_kt] * n_pages), *([cache_vt] * n_pages))


def _conv_silu(ext_ref, first, rows, cw_ref):
    acc = None
    for w in range(CONV_WIDTH):
        term = ext_ref[first + w:first + w + rows, :] * cw_ref[w:w + 1, :]
        acc = term if acc is None else acc + term
    return _silu(acc)


def _l2norm(x):
    return x * lax.rsqrt(jnp.sum(x * x, axis=-1, keepdims=True) + L2_EPS)


def _gate_terms(ab, alog_ref, dtb_ref):
    g = -jnp.exp(alog_ref[...]) * _softplus(ab + dtb_ref[...])
    return g, _sigmoid(ab)


def _gated_out_norm(o, z, gw):
    o = o * lax.rsqrt(jnp.mean(o * o, axis=-1, keepdims=True) + NORM_EPS) * gw
    return o * _silu(z)


def _gdn_prep_kernel(u_ref, prev_ref, ab_ref, cw_ref, alog_ref, dtb_ref, lincl_ref,
                     un_ref, wn_ref, qg_ref, kg_ref, att_ref, cg_ref, ext_ref):
    rows = u_ref.shape[1]
    c_len = GDN_CHUNK
    dk = GDN_HEAD_DIM
    first_block = pl.program_id(1) == 0
    prev = prev_ref[0]
    ext_ref[0:SUBLANES, :] = jnp.where(first_block, jnp.zeros_like(prev), prev)
    ext_ref[SUBLANES:, :] = u_ref[0]
    conv = _conv_silu(ext_ref, SUBLANES - (CONV_WIDTH - 1), rows, cw_ref)

    g_all, beta_all = _gate_terms(ab_ref[0], alog_ref, dtb_ref)
    r_i = lax.broadcasted_iota(jnp.int32, (c_len, c_len), 0)
    c_i = lax.broadcasted_iota(jnp.int32, (c_len, c_len), 1)
    incl = c_i <= r_i
    strict = c_i < r_i
    lincl = lincl_ref[...]
    n_chunks = rows // c_len

    cg_chunks = [_dot_exact01(lincl, g_all[c * c_len:(c + 1) * c_len, :]) for c in range(n_chunks)]
    cg_t_pairs = []
    for pair in range(n_chunks // 2):
        cg_pair = jnp.concatenate(cg_chunks[2 * pair:2 * pair + 2], axis=0)
        cg_ref[0, 2 * pair * c_len:(2 * pair + 2) * c_len, :] = cg_pair
        cg_t_pairs.append(cg_pair.T)

    problems = [(c, h) for c in range(n_chunks) for h in range(GDN_HEADS)]

    def head_cols(base, h):
        return slice(base + h * dk, base + (h + 1) * dk)

    qs, ks, vs, bcs, cgcols, decays = [], [], [], [], [], []
    for c, h in problems:
        r = slice(c * c_len, (c + 1) * c_len)
        qs.append(_l2norm(conv[r, head_cols(0, h)]) * (dk ** -0.5))
        ks.append(_l2norm(conv[r, head_cols(GDN_DIM, h)]))
        vs.append(conv[r, head_cols(2 * GDN_DIM, h)])
        bcs.append(beta_all[r, GDN_HEADS + h:GDN_HEADS + h + 1])
        cg_col = cg_chunks[c][:, h:h + 1]
        cg_row = cg_t_pairs[c // 2][h:h + 1, (c % 2) * c_len:(c % 2 + 1) * c_len]
        cgcols.append(cg_col)
        decays.append(jnp.exp(jnp.where(incl, cg_col - cg_row, -jnp.inf)))
    kbs = [k.astype(BF16) for k in ks]
    kks = [_dot_nt(kb, kb) for kb in kbs]

    powers = [jnp.where(strict, -(bc * kk * dec), 0.0) for bc, kk, dec in zip(bcs, kks, decays)]
    inv_off = powers
    for _ in range(int(math.log2(c_len)) - 1):
        pbs = [p.astype(BF16) for p in powers]
        powers = [_dot(pb, pb) for pb in pbs]
        next_pbs = [p.astype(BF16) for p in powers]
        inv_off = [t + p + _dot(t.astype(BF16), pb) for t, p, pb in zip(inv_off, powers, next_pbs)]

    for (c, h), q, k, v, bc, cg_col, dec, t_off in zip(problems, qs, ks, vs, bcs, cgcols, decays, inv_off):
        r = slice(c * c_len, (c + 1) * c_len)
        cols = head_cols(0, h)
        rhs = jnp.concatenate([v * bc, k * (bc * jnp.exp(cg_col))], axis=1)
        sol = rhs + _dot(t_off.astype(BF16), rhs.astype(BF16))
        un_ref[0, r, cols] = sol[:, :dk]
        wn_ref[0, r, cols] = sol[:, dk:].astype(BF16)
        qg_ref[0, r, cols] = (q * jnp.exp(cg_col)).astype(BF16)
        g_last = cg_chunks[c][c_len - 1:c_len, h:h + 1]
        kg_ref[0, r, cols] = (k * jnp.exp(g_last - cg_col)).astype(BF16)
        qk = _dot_nt(q.astype(BF16), k.astype(BF16))
        att_ref[0, h, r, :] = (qk * dec).astype(BF16)


def _gdn_prep(u, ab, conv_w, alog_pad, dtb_pad, lincl):
    bsz, t_len, _ = u.shape
    rows = min(PREP_ROWS, t_len)
    nb = rows // SUBLANES
    blk = lambda width: pl.BlockSpec((1, rows, width), lambda b, i: (b, i, 0))
    out_shapes = (
        jax.ShapeDtypeStruct((bsz, t_len, GDN_DIM), F32),
        jax.ShapeDtypeStruct((bsz, t_len, GDN_DIM), BF16),
        jax.ShapeDtypeStruct((bsz, t_len, GDN_DIM), BF16),
        jax.ShapeDtypeStruct((bsz, t_len, GDN_DIM), BF16),
        jax.ShapeDtypeStruct((bsz, GDN_HEADS, t_len, GDN_CHUNK), BF16),
        jax.ShapeDtypeStruct((bsz, t_len, LANES), F32),
    )
    return pl.pallas_call(
        _gdn_prep_kernel,
        grid=(bsz, t_len // rows),
        in_specs=[
            blk(CONV_DIM),
            pl.BlockSpec((1, SUBLANES, CONV_DIM), lambda b, i: (b, jnp.maximum(i * nb - 1, 0), 0)),
            blk(LANES),
            _const_spec((CONV_WIDTH, CONV_DIM)),
            _const_spec((1, LANES)),
            _const_spec((1, LANES)),
            _const_spec((GDN_CHUNK, GDN_CHUNK)),
        ],
        out_specs=(blk(GDN_DIM), blk(GDN_DIM), blk(GDN_DIM), blk(GDN_DIM),
                   pl.BlockSpec((1, GDN_HEADS, rows, GDN_CHUNK), lambda b, i: (b, 0, i, 0)),
                   blk(LANES)),
        out_shape=out_shapes,
        scratch_shapes=[pltpu.VMEM((rows + SUBLANES, CONV_DIM), F32)],
        compiler_params=_params("parallel", "parallel"),
        name="gdn_prep",
    )(u, u, ab, conv_w, alog_pad, dtb_pad, lincl)


def _gdn_scan_kernel(un_ref, wn_ref, qg_ref, kg_ref, att_ref, cg_ref, z_ref, gw_ref, o_ref, s_out_ref, s_ref):
    c = pl.program_id(0)
    bsz = un_ref.shape[0]
    c_len = GDN_CHUNK
    dk = GDN_HEAD_DIM

    @pl.when(c == 0)
    def _():
        s_ref[...] = jnp.zeros_like(s_ref)

    chains = [(b, h, slice(h * dk, (h + 1) * dk)) for b in range(bsz) for h in range(GDN_HEADS)]
    states = [s_ref[b, h] for b, h, _ in chains]
    for sub in range(un_ref.shape[1] // c_len):
        r = slice(sub * c_len, (sub + 1) * c_len)
        decay_last = [jnp.exp(cg_ref[b, r.stop - 1:r.stop, :]) for b in range(bsz)]
        sbs = [s.astype(BF16) for s in states]
        ws = [_dot(wn_ref[b, r, cols], sb) for (b, _, cols), sb in zip(chains, sbs)]
        qs = [_dot(qg_ref[b, r, cols], sb) for (b, _, cols), sb in zip(chains, sbs)]
        vbs = [(un_ref[b, r, cols] - w).astype(BF16) for (b, _, cols), w in zip(chains, ws)]
        new_states = []
        for (b, h, cols), s, q_s, vb in zip(chains, states, qs, vbs):
            o = q_s + _dot(att_ref[b, h, r, :], vb)
            new_states.append(s * decay_last[b][:, h:h + 1] + _dot_tn(kg_ref[b, r, cols], vb))
            o_ref[b, r, cols] = _gated_out_norm(o, z_ref[b, r, cols].astype(F32), gw_ref[...]).astype(BF16)
        states = new_states
    for (b, h, _), s in zip(chains, states):
        s_ref[b, h] = s

    @pl.when(c == pl.num_programs(0) - 1)
    def _():
        s_out_ref[...] = s_ref[...]


def _gdn_scan(un, wn, qg, kg, att, cg, z, gw):
    bsz, t_len, _ = un.shape
    rows = min(SCAN_ROWS, t_len)
    blk = lambda width: pl.BlockSpec((bsz, rows, width), lambda c: (0, c, 0))
    state_shape = (bsz, GDN_HEADS, GDN_HEAD_DIM, GDN_HEAD_DIM)
    return pl.pallas_call(
        _gdn_scan_kernel,
        grid=(t_len // rows,),
        in_specs=[blk(GDN_DIM), blk(GDN_DIM), blk(GDN_DIM), blk(GDN_DIM),
                  pl.BlockSpec((bsz, GDN_HEADS, rows, GDN_CHUNK), lambda c: (0, 0, c, 0)),
                  blk(LANES), blk(GDN_DIM), _const_spec((1, GDN_HEAD_DIM))],
        out_specs=(blk(GDN_DIM), pl.BlockSpec(state_shape, lambda c: (0, 0, 0, 0))),
        out_shape=(jax.ShapeDtypeStruct((bsz, t_len, GDN_DIM), BF16),
                   jax.ShapeDtypeStruct(state_shape, F32)),
        scratch_shapes=[pltpu.VMEM(state_shape, F32)],
        compiler_params=_params("arbitrary"),
        name="gdn_scan",
    )(un, wn, qg, kg, att, cg, z, gw)


def _gdn_decode_kernel(t_new, u_ref, hist_ref, ab_ref, z_ref, s_in_ref, cw_ref, alog_ref, dtb_ref, gw_ref,
                       o_ref, s_out_ref, ext_ref, ab_scr):
    dk = GDN_HEAD_DIM
    n_hist = CONV_WIDTH - 1
    n_seq = u_ref.shape[0]
    tile = (SUBLANES, LANES)
    tail = jnp.zeros((LANES - SUBLANES, LANES), F32)
    pad_rows = lambda x: jnp.concatenate([x, tail], axis=0)
    row = lax.broadcasted_iota(jnp.int32, tile, 0)
    lane = lax.broadcasted_iota(jnp.int32, tile, 1)
    real = row < t_new
    ext_ref[...] = jnp.zeros_like(ext_ref)
    ab_scr[...] = jnp.zeros_like(ab_scr)

    problems = [(g, h) for g in range(n_seq) for h in range(GDN_HEADS)]
    q8, k8, v8, e_col, b_col, w_col, decay, e_last = {}, {}, {}, {}, {}, {}, {}, {}
    for g in range(n_seq):
        ext_ref[g, 0:n_hist, :] = hist_ref[g]
        ext_ref[g, n_hist:n_hist + t_new, :] = u_ref[g]
        ab_scr[g, 0:t_new, :] = ab_ref[g]
        conv = _conv_silu(ext_ref.at[g], 0, SUBLANES, cw_ref)
        g_all, beta_all = _gate_terms(ab_scr[g], alog_ref, dtb_ref)
        cg = g_all
        for shift in range(1, t_new):
            cg = cg + jnp.where(row >= shift, pltpu.roll(g_all, shift, axis=0), 0.0)
        cg_last = cg[t_new - 1:t_new, :]
        e_all = jnp.exp(cg)
        w_all = jnp.exp(cg_last - cg)
        cg_t = pad_rows(cg).T
        for h in range(GDN_HEADS):
            p = (g, h)
            q8[p] = _l2norm(conv[:, h * dk:(h + 1) * dk]) * (dk ** -0.5)
            k8[p] = _l2norm(conv[:, GDN_DIM + h * dk:GDN_DIM + (h + 1) * dk])
            v8[p] = conv[:, 2 * GDN_DIM + h * dk:2 * GDN_DIM + (h + 1) * dk]
            e_col[p] = e_all[:, h:h + 1]
            b_col[p] = beta_all[:, GDN_HEADS + h:GDN_HEADS + h + 1]
            w_col[p] = w_all[:, h:h + 1]
            e_last[p] = e_all[t_new - 1:t_new, h:h + 1]
            decay[p] = jnp.exp(jnp.where(lane <= row, cg[:, h:h + 1] - cg_t[h:h + 1, :], -jnp.inf))

    states = {p: s_in_ref[p[0], p[1]] for p in problems}
    kq = {p: jnp.concatenate([k8[p], q8[p]], axis=0).astype(BF16) for p in problems}
    ks0_qs0 = {p: _dot(kq[p], states[p].astype(BF16)) for p in problems}
    gram = {p: _dot_nt(kq[p], pad_rows(k8[p]).astype(BF16)) for p in problems}
    deltas = {}
    for p in problems:
        m = b_col[p] * jnp.where(lane < row, gram[p][:SUBLANES] * decay[p], 0.0)
        r = b_col[p] * (v8[p] - e_col[p] * ks0_qs0[p][:SUBLANES])
        for j in range(t_new - 1):
            r = r - m[:, j:j + 1] * r[j:j + 1, :]
        deltas[p] = jnp.where(real, r, 0.0)
    updates = {}
    for p in problems:
        kw_t = pad_rows(jnp.where(real, k8[p] * w_col[p], 0.0)).T
        updates[p] = _dot(kw_t.astype(BF16), pad_rows(deltas[p]).astype(BF16))
    for g, h in problems:
        p = (g, h)
        cols = slice(h * dk, (h + 1) * dk)
        coef = gram[p][SUBLANES:] * decay[p]
        o = e_col[p] * ks0_qs0[p][SUBLANES:]
        for j in range(t_new):
            o = o + coef[:, j:j + 1] * deltas[p][j:j + 1, :]
        o = _gated_out_norm(o[:t_new], z_ref[g, :, cols].astype(F32), gw_ref[...])
        o_ref[g, :, cols] = o.astype(BF16)
        s_out_ref[g, h] = states[p] * e_last[p] + updates[p]


def _gdn_decode(u, hist, ab, z, s_in, conv_w, alog_pad, dtb_pad, gw):
    n_seq, t_new, _ = u.shape
    grp = DEC_GROUP
    seq = lambda r, width: pl.BlockSpec((grp, r, width), lambda i: (i, 0, 0))
    state = pl.BlockSpec((grp, GDN_HEADS, GDN_HEAD_DIM, GDN_HEAD_DIM), lambda i: (i, 0, 0, 0))
    return pl.pallas_call(
        functools.partial(_gdn_decode_kernel, t_new),
        grid=(n_seq // grp,),
        in_specs=[seq(t_new, CONV_DIM), seq(CONV_WIDTH - 1, CONV_DIM), seq(t_new, LANES), seq(t_new, GDN_DIM),
                  state, _const_spec((CONV_WIDTH, CONV_DIM)), _const_spec((1, LANES)), _const_spec((1, LANES)),
                  _const_spec((1, GDN_HEAD_DIM))],
        out_specs=(seq(t_new, GDN_DIM), state),
        out_shape=(jax.ShapeDtypeStruct((n_seq, t_new, GDN_DIM), BF16),
                   jax.ShapeDtypeStruct(s_in.shape, F32)),
        scratch_shapes=[pltpu.VMEM((grp, 2 * SUBLANES, CONV_DIM), F32), pltpu.VMEM((grp, SUBLANES, LANES), F32)],
        compiler_params=_params("parallel"),
        name="gdn_decode",
    )(u, hist, ab, z, s_in, conv_w, alog_pad, dtb_pad, gw)


def _merge_kernel(x_ref, oa_ref, ob_ref, sg_ref, wpa_ref, wpb_ref, wo_ref, nw_ref, x1_ref, hm_ref):
    y_a = _dot(oa_ref[...], wpa_ref[...])
    y_b = _dot(ob_ref[...], wpb_ref[...])
    mix = sg_ref[:, :D_MODEL] * y_a + sg_ref[:, D_MODEL:] * y_b
    x1 = x_ref[...] + _dot(mix.astype(BF16), wo_ref[...])
    x1_ref[...] = x1
    hm_ref[...] = _rmsnorm(x1, nw_ref[...]).astype(BF16)


def _merge(x2d, o_a, o_b, sg, w_pa, w_pb, w_o, norm_w, tm):
    n = x2d.shape[0]
    row = lambda width: pl.BlockSpec((tm, width), lambda i: (i, 0))
    return pl.pallas_call(
        _merge_kernel,
        grid=(n // tm,),
        in_specs=[row(D_MODEL), row(SB_DIM), row(GDN_DIM), row(2 * D_MODEL),
                  _const_spec(w_pa.shape), _const_spec(w_pb.shape), _const_spec(w_o.shape),
                  _const_spec((1, D_MODEL))],
        out_specs=(row(D_MODEL), row(D_MODEL)),
        out_shape=(jax.ShapeDtypeStruct((n, D_MODEL), F32), jax.ShapeDtypeStruct((n, D_MODEL), BF16)),
        compiler_params=_params("parallel"),
        name="merge",
    )(x2d, o_a, o_b, sg, w_pa, w_pb, w_o, norm_w)


def _mlp_kernel(x1_ref, hm_ref, wup_ref, wdown_ref, nf_ref, y_ref, acc_ref):
    f = pl.program_id(1)

    @pl.when(f == 0)
    def _():
        acc_ref[...] = x1_ref[...]

    up = jnp.maximum(_dot(hm_ref[...], wup_ref[...]), 0.0)
    acc_ref[...] += _dot((up * up).astype(BF16), wdown_ref[...])

    @pl.when(f == pl.num_programs(1) - 1)
    def _():
        y_ref[...] = _rmsnorm(acc_ref[...], nf_ref[...])


def _mlp(x1, hm, w_up, w_down, norm_f, tm, tf):
    n = x1.shape[0]
    return pl.pallas_call(
        _mlp_kernel,
        grid=(n // tm, D_FF // tf),
        in_specs=[pl.BlockSpec((tm, D_MODEL), lambda i, f: (i, 0)),
                  pl.BlockSpec((tm, D_MODEL), lambda i, f: (i, 0)),
                  pl.BlockSpec((D_MODEL, tf), lambda i, f: (0, f)),
                  pl.BlockSpec((tf, D_MODEL), lambda i, f: (f, 0)),
                  _const_spec((1, D_MODEL))],
        out_specs=pl.BlockSpec((tm, D_MODEL), lambda i, f: (i, 0)),
        out_shape=jax.ShapeDtypeStruct((n, D_MODEL), F32),
        scratch_shapes=[pltpu.VMEM((tm, D_MODEL), F32)],
        compiler_params=_params("parallel", "arbitrary"),
        name="mlp",
    )(x1, hm, w_up, w_down, norm_f)


def _permute_w_in(w):
    n_main = 3 * SB_DIM + CONV_DIM + GDN_DIM
    gates = w[:, n_main + 2 * GDN_HEADS:]
    ab = w[:, n_main:n_main + 2 * GDN_HEADS]
    pad = jnp.zeros((w.shape[0], LANES - 2 * GDN_HEADS), w.dtype)
    return jnp.concatenate([w[:, :n_main], gates, ab, pad], axis=1).astype(BF16)


def _lane_pad(vec, offset):
    return jnp.zeros((1, LANES), F32).at[0, offset:offset + vec.shape[0]].set(vec.astype(F32))


def _layer_common(x2d, lw, tm, page=None):
    return _inproj(x2d, lw["norm_mix_w"], lw["w_in"], tm, page)


def _layer_tail(x2d, o_a, o_b, sg, lw, norm_f, tm, tm_mlp, tf):
    x1, hm = _merge(x2d, o_a, o_b, sg, lw["w_pa"], lw["w_pb"], lw["w_o"], lw["norm_mlp_w"], tm)
    return _mlp(x1, hm, lw["w_up"], lw["w_down"], norm_f, tm_mlp, tf)


def kernel(x_prompt, x_sample, cache_k, cache_v, page_table, state_conv, state_ssm, norm_mix_w, w_in, sb_bias,
           conv_w, a_log, dt_bias, gdn_norm_w, w_pa, w_pb, w_o, norm_mlp_w, w_up, w_down, norm_final_w):
    depth = w_in.shape[0]
    assert depth == 1, "the residual stream is normalised once, after the only layer"
    b_p, t_p, _ = x_prompt.shape
    b_s, t_s, _ = x_sample.shape
    n_pool, page = cache_k.shape[1], cache_k.shape[2]
    l = 0
    lw = {
        "norm_mix_w": norm_mix_w[l].reshape(1, D_MODEL).astype(F32),
        "w_in": _permute_w_in(w_in[l]),
        "w_pa": w_pa[l].astype(BF16), "w_pb": w_pb[l].astype(BF16), "w_o": w_o[l].astype(BF16),
        "norm_mlp_w": norm_mlp_w[l].reshape(1, D_MODEL).astype(F32),
        "w_up": w_up[l].astype(BF16), "w_down": w_down[l].astype(BF16),
    }
    norm_f = norm_final_w.reshape(1, D_MODEL).astype(F32)
    bias = sb_bias[l].astype(F32)
    cw = conv_w[l].astype(F32)
    alog_pad = _lane_pad(a_log[l], 0)
    dtb_pad = _lane_pad(dt_bias[l], 0)
    gw = gdn_norm_w[l].reshape(1, GDN_HEAD_DIM).astype(F32)
    ntri = -jnp.tril(jnp.ones((SB_BLOCK, SB_BLOCK), BF16), -1)
    lincl = jnp.tril(jnp.ones((GDN_CHUNK, GDN_CHUNK), BF16))

    xp = x_prompt.reshape(b_p * t_p, D_MODEL).astype(F32)
    q, k_t, v_t, kb, vb, u, z, ab, sg = _layer_common(xp, lw, 256, page)
    shp = lambda a: a.reshape(b_p, t_p, a.shape[-1])
    o_a = _sb_prompt(bias, shp(q), shp(kb), shp(vb), ntri)
    un, wn, qg, kg, att, cg = _gdn_prep(shp(u), shp(ab), cw, alog_pad, dtb_pad, lincl)
    o_b, ssm_p = _gdn_scan(un, wn, qg, kg, att, cg, shp(z), gw)
    y_p = _layer_tail(xp, o_a.reshape(-1, SB_DIM), o_b.reshape(-1, GDN_DIM), sg, lw, norm_f, 512, 1024, 512)
    y_prompt = y_p.reshape(b_p, t_p, D_MODEL).astype(x_prompt.dtype)
    as_pages = lambda a_t: jnp.transpose(
        a_t.reshape(b_p, t_p // page, SB_HEADS, SB_HEAD_DIM, page), (0, 1, 4, 2, 3))[None]
    new_k_prompt = as_pages(k_t).astype(cache_k.dtype)
    new_v_prompt = as_pages(v_t).astype(cache_v.dtype)
    new_conv_prompt = shp(u)[None, :, t_p - (CONV_WIDTH - 1):, :].astype(state_conv.dtype)
    new_ssm_prompt = ssm_p[None].astype(state_ssm.dtype)

    xs = x_sample.reshape(b_s * t_s, D_MODEL).astype(F32)
    tm_s = min(256, b_s * t_s)
    q, k, v, _, _, u, z, ab, sg = _layer_common(xs, lw, tm_s)
    shs = lambda a: a.reshape(b_s, t_s, a.shape[-1])
    pad8 = lambda a: jnp.pad(shs(a), ((0, 0), (0, SUBLANES - t_s), (0, 0)))
    pages_t = lambda c: jnp.transpose(c[l], (0, 2, 3, 1)).reshape(n_pool, SB_DIM, page).astype(F32)
    o_a = _sb_decode(page_table, bias, shs(q), pad8(k), pad8(v), ntri, pages_t(cache_k), pages_t(cache_v))
    o_b, ssm_s = _gdn_decode(shs(u), state_conv[l].astype(F32), shs(ab), shs(z), state_ssm[l].astype(F32),
                             cw, alog_pad, dtb_pad, gw)
    y_s = _layer_tail(xs, o_a.reshape(-1, SB_DIM), o_b.reshape(-1, GDN_DIM), sg, lw, norm_f, tm_s, tm_s, 512)
    y_sample = y_s.reshape(b_s, t_s, D_MODEL).astype(x_sample.dtype)
    head_shape = (depth, b_s, t_s, SB_HEADS, SB_HEAD_DIM)
    new_k_sample = k.reshape(head_shape).astype(cache_k.dtype)
    new_v_sample = v.reshape(head_shape).astype(cache_v.dtype)
    new_conv_sample = jnp.concatenate([state_conv[l].astype(F32), shs(u)], axis=1)[None, :, t_s:, :].astype(state_conv.dtype)
    new_ssm_sample = ssm_s[None].astype(state_ssm.dtype)

    return (y_prompt, y_sample, new_k_prompt, new_v_prompt, new_k_sample, new_v_sample,
            new_conv_prompt, new_conv_sample, new_ssm_prompt, new_ssm_sample)
```

```python
import functools
import math

import jax
import jax.numpy as jnp
from jax import lax
from jax.experimental import pallas as pl
from jax.experimental.pallas import tpu as pltpu

F32 = jnp.float32
BF16 = jnp.bfloat16

D_MODEL = 1024
SB_HEADS = 8
SB_HEAD_DIM = 64
SB_DIM = SB_HEADS * SB_HEAD_DIM
GDN_HEADS = 4
GDN_HEAD_DIM = 128
GDN_DIM = GDN_HEADS * GDN_HEAD_DIM
CONV_WIDTH = 4
CONV_DIM = 3 * GDN_DIM
GDN_CHUNK = 64
D_FF = 4 * D_MODEL
NORM_EPS = 1e-6
L2_EPS = 1e-6
NEG_LOG2E = -1.4426950408889634

LANES = 128
SUBLANES = 8
VMEM_LIMIT_BYTES = 56 * 1024 * 1024

COL_Q = 0
COL_K = COL_Q + SB_DIM
COL_V = COL_K + SB_DIM
COL_U = COL_V + SB_DIM
COL_Z = COL_U + CONV_DIM
COL_GA = COL_Z + GDN_DIM
COL_GB = COL_GA + D_MODEL
COL_AB = COL_GB + D_MODEL
IN_COLS = COL_AB + LANES

SB_BLOCK = 256
SB_ROW_CHUNK = 64
PREP_ROWS = 512
SCAN_ROWS = 128
MERGE_ROWS = 512
DEC_GROUP = 8


def _dot(a, b):
    return jnp.dot(a, b, preferred_element_type=F32)


def _dot_nt(a, b):
    return lax.dot_general(a, b, (((1,), (1,)), ((), ())), preferred_element_type=F32)


def _dot_tn(a, b):
    return lax.dot_general(a, b, (((0,), (0,)), ((), ())), preferred_element_type=F32)


def _dot_exact01(m01, x):
    h = x.astype(BF16)
    r = x - h.astype(F32)
    m = r.astype(BF16)
    l = (r - m.astype(F32)).astype(BF16)
    return _dot(m01, h) + (_dot(m01, m) + _dot(m01, l))


def _softplus(z):
    return jnp.maximum(z, 0.0) + jnp.log(1.0 + jnp.exp2(jnp.abs(z) * NEG_LOG2E))


def _sigmoid(z):
    return 1.0 / (1.0 + jnp.exp(-z))


def _silu(z):
    return z * _sigmoid(z)


def _rmsnorm(x, w):
    return x * lax.rsqrt(jnp.mean(x * x, axis=-1, keepdims=True) + NORM_EPS) * w


def _const_spec(shape):
    nd = len(shape)
    return pl.BlockSpec(shape, lambda *_: (0,) * nd, pipeline_mode=pl.Buffered(1))


def _params(*sem):
    return pltpu.CompilerParams(dimension_semantics=sem, vmem_limit_bytes=VMEM_LIMIT_BYTES)


def _inproj_kernel(page, x_ref, nw_ref, w_ref, q_ref, k_ref, v_ref, kb_ref, vb_ref, u_ref, z_ref, ab_ref, sg_ref):
    x = x_ref[...]
    hb = _rmsnorm(x, nw_ref[...]).astype(BF16)

    def seg(lo, width):
        return _dot(hb, w_ref[:, lo:lo + width])

    def store_kv(ref, val):
        if page is None:
            ref[...] = val
        else:
            val_t = val.T
            for p in range(val.shape[0] // page):
                ref[p] = val_t[:, p * page:(p + 1) * page]

    q_ref[...] = (seg(COL_Q, SB_DIM) * (SB_HEAD_DIM ** -0.5)).astype(BF16)
    k = seg(COL_K, SB_DIM)
    store_kv(k_ref, k)
    kb_ref[...] = k.astype(BF16)
    v = seg(COL_V, SB_DIM)
    store_kv(v_ref, v)
    vb_ref[...] = v.astype(BF16)
    for j in range(CONV_DIM // SB_DIM):
        u_ref[:, j * SB_DIM:(j + 1) * SB_DIM] = seg(COL_U + j * SB_DIM, SB_DIM)
    z_ref[...] = seg(COL_Z, GDN_DIM).astype(BF16)
    for j in range(2 * D_MODEL // SB_DIM):
        sg_ref[:, j * SB_DIM:(j + 1) * SB_DIM] = _sigmoid(seg(COL_GA + j * SB_DIM, SB_DIM)).astype(BF16)
    ab_ref[...] = seg(COL_AB, LANES)


def _inproj(x2d, norm_w, w_perm, tm, page=None):
    n = x2d.shape[0]
    row = lambda width: pl.BlockSpec((tm, width), lambda i: (i, 0))
    if page is None:
        kv_shape, kv_spec = jax.ShapeDtypeStruct((n, SB_DIM), F32), row(SB_DIM)
    else:
        kv_shape = jax.ShapeDtypeStruct((n // page, SB_DIM, page), F32)
        kv_spec = pl.BlockSpec((tm // page, SB_DIM, page), lambda i: (i, 0, 0))
    out_shapes = (
        jax.ShapeDtypeStruct((n, SB_DIM), BF16),
        kv_shape,
        kv_shape,
        jax.ShapeDtypeStruct((n, SB_DIM), BF16),
        jax.ShapeDtypeStruct((n, SB_DIM), BF16),
        jax.ShapeDtypeStruct((n, CONV_DIM), F32),
        jax.ShapeDtypeStruct((n, GDN_DIM), BF16),
        jax.ShapeDtypeStruct((n, LANES), F32),
        jax.ShapeDtypeStruct((n, 2 * D_MODEL), BF16),
    )
    out_specs = tuple(kv_spec if i in (1, 2) else row(s.shape[1]) for i, s in enumerate(out_shapes))
    return pl.pallas_call(
        functools.partial(_inproj_kernel, page),
        grid=(n // tm,),
        in_specs=[row(D_MODEL), _const_spec((1, D_MODEL)), _const_spec((D_MODEL, IN_COLS))],
        out_specs=out_specs,
        out_shape=out_shapes,
        compiler_params=_params("parallel"),
        name="inproj",
    )(x2d, norm_w, w_perm)


def _sb_tile(z, carry, ntri, mask):
    sp = _softplus(z)
    if mask is not None:
        sp = jnp.where(mask, sp, 0.0)
    suffix = _dot(sp.astype(BF16), ntri)
    w = jnp.exp(z - sp + suffix + carry)
    if mask is not None:
        w = jnp.where(mask, w, 0.0)
    return w.astype(BF16), carry - jnp.sum(sp, axis=1, keepdims=True)


def _sb_prompt_kernel(bias_ref, q_ref, k_ref, v_ref, ntri_ref, o_ref,
                      qs_scr, bias_scr, z_scr, beta_scr, decay_scr, run_scr, acc_scr):
    hp = pl.program_id(1)
    tb = SB_BLOCK
    n_q = q_ref.shape[1] // tb
    n_tiles = n_q * (n_q + 1) // 2
    lane = lax.broadcasted_iota(jnp.int32, (1, LANES), 1)
    head_lanes = [lane < SB_HEAD_DIM, lane >= SB_HEAD_DIM]
    zero = jnp.zeros((), BF16)
    rows = [slice(j * tb, (j + 1) * tb) for j in range(2)]

    for ref in (z_scr, beta_scr, decay_scr, run_scr, acc_scr):
        ref[...] = jnp.zeros_like(ref)
    row = lax.broadcasted_iota(jnp.int32, (tb, tb), 0)
    col = lax.broadcasted_iota(jnp.int32, (tb, tb), 1)
    for j in range(2):
        bias = bias_ref[2 * hp + j]
        bias_scr[0, j] = jnp.full((tb, tb), bias, F32)
        bias_scr[1, j] = jnp.where(col < row, bias, -1e30)
    for i in range(n_q):
        q = q_ref[0, i * tb:(i + 1) * tb, :]
        for j in range(2):
            qs_scr[i, rows[j], :] = jnp.where(head_lanes[j], q, zero)

    def key_block(ref, kb):
        return ref[0, pl.ds(pl.multiple_of(kb * tb, tb), tb), :]

    def step(s, tiles):
        (q_a, k_a), (q_b, k_b), (q_d, k_d) = tiles
        chunks = [slice(c * SB_ROW_CHUNK, (c + 1) * SB_ROW_CHUNK) for c in range(2 * tb // SB_ROW_CHUNK)]
        zs = [z_scr[r, :] for r in chunks]
        z_scr[...] = _dot_nt(qs_scr[q_a], key_block(k_ref, k_a))
        valid_d = jnp.logical_and(s >= 2, s - 2 < n_tiles)
        v = key_block(v_ref, k_d)
        pv = None
        for j in range(2):
            vm = jnp.where(jnp.logical_and(head_lanes[j], valid_d), v, zero)
            w = beta_scr[rows[j], :] * decay_scr[rows[j], :]
            term = _dot(w, vm)
            pv = term if pv is None else pv + term
        acc_rows = pl.ds(pl.multiple_of(q_d * tb, tb), tb)
        acc_scr[acc_rows, :] = acc_scr[acc_rows, :] + pv
        first_of_block = k_b == q_b
        offset = first_of_block.astype(jnp.int32)
        one = jnp.ones((), BF16)
        sps = []
        for r, z_raw in zip(chunks, zs):
            j, r_in_head = divmod(r.start, tb)
            z = z_raw + bias_scr[offset, j, r_in_head:r_in_head + SB_ROW_CHUNK, :]
            t = jnp.exp2(jnp.abs(z) * NEG_LOG2E)
            log1p_t = jnp.log(1.0 + t).astype(BF16)
            z16, t16 = z.astype(BF16), t.astype(BF16)
            sps.append(jnp.maximum(z16, zero) + log1p_t)
            beta_scr[r, :] = jnp.where(z16 >= zero, one, t16) / (one + t16)
        suffix = _dot(jnp.concatenate(sps, axis=0), ntri_ref[...])
        for r, sp in zip(chunks, sps):
            run = jnp.where(first_of_block, 0.0, run_scr[r, :])
            decay_scr[r, :] = jnp.exp(suffix[r, :] + run).astype(BF16)
            run_scr[r, :] = run + suffix[r, 0:1] - sp[:, 0:1].astype(F32)
        wrap = k_a == 0
        q_n = jnp.where(wrap, jnp.minimum(q_a + 1, n_q - 1), q_a)
        k_n = jnp.where(wrap, q_n, k_a - 1)
        return ((q_n, k_n),) + tiles[:2]

    first = (jnp.int32(0), jnp.int32(0))
    lax.fori_loop(0, n_tiles + 2, step, (first,) * 3)
    o_ref[0] = acc_scr[...].astype(BF16)


def _sb_prompt(sb_bias, q, k, v, ntri):
    bsz, t_len, _ = q.shape
    tb = SB_BLOCK
    seq = pl.BlockSpec((1, t_len, LANES), lambda b, h, *_: (b, 0, h))
    return pl.pallas_call(
        _sb_prompt_kernel,
        grid_spec=pltpu.PrefetchScalarGridSpec(
            num_scalar_prefetch=1,
            grid=(bsz, SB_DIM // LANES),
            in_specs=[seq, seq, seq, pl.BlockSpec((tb, tb), lambda b, h, *_: (0, 0))],
            out_specs=seq,
            scratch_shapes=[
                pltpu.VMEM((t_len // tb, 2 * tb, LANES), BF16),
                pltpu.VMEM((2, 2, tb, tb), F32),
                pltpu.VMEM((2 * tb, tb), F32),
                pltpu.VMEM((2 * tb, tb), BF16),
                pltpu.VMEM((2 * tb, tb), BF16),
                pltpu.VMEM((2 * tb, 1), F32),
                pltpu.VMEM((t_len, LANES), F32),
            ],
        ),
        out_shape=jax.ShapeDtypeStruct((bsz, t_len, SB_DIM), BF16),
        compiler_params=_params("parallel", "parallel"),
        name="sb_prompt",
    )(sb_bias, q, k, v, ntri)


def _sb_decode_kernel(n_pages, t_new, pt_ref, bias_ref, q_ref, kn_ref, vn_ref, ntri_ref, *refs):
    kt_pages = refs[:n_pages]
    vt_pages = refs[n_pages:2 * n_pages]
    o_ref = refs[2 * n_pages]
    page = kt_pages[0].shape[2]
    rows = t_new * SB_HEADS
    r_iota = lax.broadcasted_iota(jnp.int32, (rows, SB_DIM), 0)
    l_iota = lax.broadcasted_iota(jnp.int32, (rows, SB_DIM), 1)
    head_of_row = r_iota % SB_HEADS
    own_lanes = (l_iota // SB_HEAD_DIM) == head_of_row

    q = q_ref[0].astype(F32)
    q_rep = jnp.broadcast_to(q[:, None, :], (t_new, SB_HEADS, SB_DIM)).reshape(rows, SB_DIM)
    q_bd = jnp.where(own_lanes, q_rep, 0.0).astype(BF16)

    r1 = lax.broadcasted_iota(jnp.int32, (rows, 1), 0)
    bias = jnp.zeros((rows, 1), F32)
    for h in range(SB_HEADS):
        bias = jnp.where(r1 % SB_HEADS == h, bias_ref[h], bias)

    pad = jnp.zeros((page - SUBLANES, SB_DIM), F32)
    k_new = jnp.concatenate([kn_ref[0], pad], axis=0).astype(BF16)
    v_new = jnp.concatenate([vn_ref[0], pad], axis=0).astype(BF16)
    kcol = lax.broadcasted_iota(jnp.int32, (rows, page), 1)
    qtok = lax.broadcasted_iota(jnp.int32, (rows, page), 0) // SB_HEADS
    z = _dot_nt(q_bd, k_new) + bias
    w, carry = _sb_tile(z, jnp.zeros((rows, 1), F32), ntri_ref[:page, :page], kcol < qtok)
    acc = _dot(w, v_new)

    blk = ntri_ref.shape[0] // page
    order = range(n_pages // blk - 1, -1, -1)
    lane_cat = lambda pages, pb: jnp.concatenate(
        [pages[pb * blk + i][0] for i in range(blk)], axis=1).astype(BF16)
    zs = [_dot(q_bd, lane_cat(kt_pages, pb)) + bias for pb in order]
    sps = [_softplus(z) for z in zs]
    suffixes = [_dot(sp.astype(BF16), ntri_ref[...]) for sp in sps]
    for pb, z, sp, suffix in zip(order, zs, sps, suffixes):
        w = jnp.exp(z - sp + suffix + carry).astype(BF16)
        carry = carry - jnp.sum(sp, axis=1, keepdims=True)
        acc = acc + _dot_nt(w, lane_cat(vt_pages, pb))

    acc = jnp.where(own_lanes, acc, 0.0)
    o_ref[0] = jnp.sum(acc.reshape(t_new, SB_HEADS, SB_DIM), axis=1).astype(BF16)


def _sb_decode(page_table, sb_bias, q, k_new8, v_new8, ntri, cache_kt, cache_vt):
    n_seq, t_new, _ = q.shape
    n_pages = page_table.shape[1]
    page = cache_kt.shape[2]

    def page_spec(p):
        return pl.BlockSpec((1, SB_DIM, page), lambda s, pt, b: (pt[s, p], 0, 0))

    seq_spec = lambda r: pl.BlockSpec((1, r, SB_DIM), lambda s, pt, b: (s, 0, 0))
    return pl.pallas_call(
        functools.partial(_sb_decode_kernel, n_pages, t_new),
        grid_spec=pltpu.PrefetchScalarGridSpec(
            num_scalar_prefetch=2,
            grid=(n_seq,),
            in_specs=[seq_spec(t_new), seq_spec(SUBLANES), seq_spec(SUBLANES),
                      pl.BlockSpec(ntri.shape, lambda s, pt, b: (0, 0))]
                     + [page_spec(p) for p in range(n_pages)] * 2,
            out_specs=seq_spec(t_new),
        ),
        out_shape=jax.ShapeDtypeStruct((n_seq, t_new, SB_DIM), BF16),
        compiler_params=_params("parallel"),
        name="sb_decode",
    )(page_table, sb_bias, q, k_new8, v_new8, ntri, *([cache_kt] * n_pages), *([cache_vt] * n_pages))


def _conv_silu(ext_ref, first, rows, cw_ref):
    acc = None
    for w in range(CONV_WIDTH):
        term = ext_ref[first + w:first + w + rows, :] * cw_ref[w:w + 1, :]
        acc = term if acc is None else acc + term
    return _silu(acc)


def _l2norm(x):
    return x * lax.rsqrt(jnp.sum(x * x, axis=-1, keepdims=True) + L2_EPS)


def _gate_terms(ab, alog_ref, dtb_ref):
    g = -jnp.exp(alog_ref[...]) * _softplus(ab + dtb_ref[...])
    return g, _sigmoid(ab)


def _gated_out_norm(o, z, gw):
    o = o * lax.rsqrt(jnp.mean(o * o, axis=-1, keepdims=True) + NORM_EPS) * gw
    return o * _silu(z)


def _gdn_prep_kernel(u_ref, prev_ref, ab_ref, cw_ref, alog_ref, dtb_ref, lincl_ref,
                     un_ref, wn_ref, qg_ref, kg_ref, att_ref, cg_ref, ext_ref):
    rows = u_ref.shape[1]
    c_len = GDN_CHUNK
    dk = GDN_HEAD_DIM
    first_block = pl.program_id(1) == 0
    prev = prev_ref[0]
    ext_ref[0:SUBLANES, :] = jnp.where(first_block, jnp.zeros_like(prev), prev)
    ext_ref[SUBLANES:, :] = u_ref[0]
    conv = _conv_silu(ext_ref, SUBLANES - (CONV_WIDTH - 1), rows, cw_ref)

    g_all, beta_all = _gate_terms(ab_ref[0], alog_ref, dtb_ref)
    r_i = lax.broadcasted_iota(jnp.int32, (c_len, c_len), 0)
    c_i = lax.broadcasted_iota(jnp.int32, (c_len, c_len), 1)
    incl = c_i <= r_i
    strict = c_i < r_i
    lincl = lincl_ref[...]
    n_chunks = rows // c_len

    cg_chunks = [_dot_exact01(lincl, g_all[c * c_len:(c + 1) * c_len, :]) for c in range(n_chunks)]
    cg_t_pairs = []
    for pair in range(n_chunks // 2):
        cg_pair = jnp.concatenate(cg_chunks[2 * pair:2 * pair + 2], axis=0)
        cg_ref[0, 2 * pair * c_len:(2 * pair + 2) * c_len, :] = cg_pair
        cg_t_pairs.append(cg_pair.T)

    problems = [(c, h) for c in range(n_chunks) for h in range(GDN_HEADS)]

    def head_cols(base, h):
        return slice(base + h * dk, base + (h + 1) * dk)

    qs, ks, vs, bcs, cgcols, decays = [], [], [], [], [], []
    for c, h in problems:
        r = slice(c * c_len, (c + 1) * c_len)
        qs.append(_l2norm(conv[r, head_cols(0, h)]) * (dk ** -0.5))
        ks.append(_l2norm(conv[r, head_cols(GDN_DIM, h)]))
        vs.append(conv[r, head_cols(2 * GDN_DIM, h)])
        bcs.append(beta_all[r, GDN_HEADS + h:GDN_HEADS + h + 1])
        cg_col = cg_chunks[c][:, h:h + 1]
        cg_row = cg_t_pairs[c // 2][h:h + 1, (c % 2) * c_len:(c % 2 + 1) * c_len]
        cgcols.append(cg_col)
        decays.append(jnp.exp(jnp.where(incl, cg_col - cg_row, -jnp.inf)))
    kbs = [k.astype(BF16) for k in ks]
    kks = [_dot_nt(kb, kb) for kb in kbs]

    powers = [jnp.where(strict, -(bc * kk * dec), 0.0) for bc, kk, dec in zip(bcs, kks, decays)]
    inv_off = powers
    for _ in range(int(math.log2(c_len)) - 1):
        pbs = [p.astype(BF16) for p in powers]
        powers = [_dot(pb, pb) for pb in pbs]
        next_pbs = [p.astype(BF16) for p in powers]
        inv_off = [t + p + _dot(t.astype(BF16), pb) for t, p, pb in zip(inv_off, powers, next_pbs)]

    for (c, h), q, k, v, bc, cg_col, dec, t_off in zip(problems, qs, ks, vs, bcs, cgcols, decays, inv_off):
        r = slice(c * c_len, (c + 1) * c_len)
        cols = head_cols(0, h)
        rhs = jnp.concatenate([v * bc, k * (bc * jnp.exp(cg_col))], axis=1)
        sol = rhs + _dot(t_off.astype(BF16), rhs.astype(BF16))
        un_ref[0, r, cols] = sol[:, :dk]
        wn_ref[0, r, cols] = sol[:, dk:].astype(BF16)
        qg_ref[0, r, cols] = (q * jnp.exp(cg_col)).astype(BF16)
        g_last = cg_chunks[c][c_len - 1:c_len, h:h + 1]
        kg_ref[0, r, cols] = (k * jnp.exp(g_last - cg_col)).astype(BF16)
        qk = _dot_nt(q.astype(BF16), k.astype(BF16))
        att_ref[0, h, r, :] = (qk * dec).astype(BF16)


def _gdn_prep(u, ab, conv_w, alog_pad, dtb_pad, lincl):
    bsz, t_len, _ = u.shape
    rows = min(PREP_ROWS, t_len)
    nb = rows // SUBLANES
    blk = lambda width: pl.BlockSpec((1, rows, width), lambda b, i: (b, i, 0))
    out_shapes = (
        jax.ShapeDtypeStruct((bsz, t_len, GDN_DIM), F32),
        jax.ShapeDtypeStruct((bsz, t_len, GDN_DIM), BF16),
        jax.ShapeDtypeStruct((bsz, t_len, GDN_DIM), BF16),
        jax.ShapeDtypeStruct((bsz, t_len, GDN_DIM), BF16),
        jax.ShapeDtypeStruct((bsz, GDN_HEADS, t_len, GDN_CHUNK), BF16),
        jax.ShapeDtypeStruct((bsz, t_len, LANES), F32),
    )
    return pl.pallas_call(
        _gdn_prep_kernel,
        grid=(bsz, t_len // rows),
        in_specs=[
            blk(CONV_DIM),
            pl.BlockSpec((1, SUBLANES, CONV_DIM), lambda b, i: (b, jnp.maximum(i * nb - 1, 0), 0)),
            blk(LANES),
            _const_spec((CONV_WIDTH, CONV_DIM)),
            _const_spec((1, LANES)),
            _const_spec((1, LANES)),
            _const_spec((GDN_CHUNK, GDN_CHUNK)),
        ],
        out_specs=(blk(GDN_DIM), blk(GDN_DIM), blk(GDN_DIM), blk(GDN_DIM),
                   pl.BlockSpec((1, GDN_HEADS, rows, GDN_CHUNK), lambda b, i: (b, 0, i, 0)),
                   blk(LANES)),
        out_shape=out_shapes,
        scratch_shapes=[pltpu.VMEM((rows + SUBLANES, CONV_DIM), F32)],
        compiler_params=_params("parallel", "parallel"),
        name="gdn_prep",
    )(u, u, ab, conv_w, alog_pad, dtb_pad, lincl)


def _gdn_scan_kernel(un_ref, wn_ref, qg_ref, kg_ref, att_ref, cg_ref, z_ref, gw_ref, o_ref, s_out_ref, s_ref):
    c = pl.program_id(0)
    bsz = un_ref.shape[0]
    c_len = GDN_CHUNK
    dk = GDN_HEAD_DIM

    @pl.when(c == 0)
    def _():
        s_ref[...] = jnp.zeros_like(s_ref)

    chains = [(b, h, slice(h * dk, (h + 1) * dk)) for b in range(bsz) for h in range(GDN_HEADS)]
    states = [s_ref[b, h] for b, h, _ in chains]
    for sub in range(un_ref.shape[1] // c_len):
        r = slice(sub * c_len, (sub + 1) * c_len)
        decay_last = [jnp.exp(cg_ref[b, r.stop - 1:r.stop, :]) for b in range(bsz)]
        sbs = [s.astype(BF16) for s in states]
        ws = [_dot(wn_ref[b, r, cols], sb) for (b, _, cols), sb in zip(chains, sbs)]
        qs = [_dot(qg_ref[b, r, cols], sb) for (b, _, cols), sb in zip(chains, sbs)]
        vbs = [(un_ref[b, r, cols] - w).astype(BF16) for (b, _, cols), w in zip(chains, ws)]
        new_states = []
        for (b, h, cols), s, q_s, vb in zip(chains, states, qs, vbs):
            o = q_s + _dot(att_ref[b, h, r, :], vb)
            new_states.append(s * decay_last[b][:, h:h + 1] + _dot_tn(kg_ref[b, r, cols], vb))
            o_ref[b, r, cols] = _gated_out_norm(o, z_ref[b, r, cols].astype(F32), gw_ref[...]).astype(BF16)
        states = new_states
    for (b, h, _), s in zip(chains, states):
        s_ref[b, h] = s

    @pl.when(c == pl.num_programs(0) - 1)
    def _():
        s_out_ref[...] = s_ref[...]


def _gdn_scan(un, wn, qg, kg, att, cg, z, gw):
    bsz, t_len, _ = un.shape
    rows = min(SCAN_ROWS, t_len)
    blk = lambda width: pl.BlockSpec((bsz, rows, width), lambda c: (0, c, 0))
    state_shape = (bsz, GDN_HEADS, GDN_HEAD_DIM, GDN_HEAD_DIM)
    return pl.pallas_call(
        _gdn_scan_kernel,
        grid=(t_len // rows,),
        in_specs=[blk(GDN_DIM), blk(GDN_DIM), blk(GDN_DIM), blk(GDN_DIM),
                  pl.BlockSpec((bsz, GDN_HEADS, rows, GDN_CHUNK), lambda c: (0, 0, c, 0)),
                  blk(LANES), blk(GDN_DIM), _const_spec((1, GDN_HEAD_DIM))],
        out_specs=(blk(GDN_DIM), pl.BlockSpec(state_shape, lambda c: (0, 0, 0, 0))),
        out_shape=(jax.ShapeDtypeStruct((bsz, t_len, GDN_DIM), BF16),
                   jax.ShapeDtypeStruct(state_shape, F32)),
        scratch_shapes=[pltpu.VMEM(state_shape, F32)],
        compiler_params=_params("arbitrary"),
        name="gdn_scan",
    )(un, wn, qg, kg, att, cg, z, gw)


def _gdn_decode_kernel(t_new, u_ref, hist_ref, ab_ref, z_ref, s_in_ref, cw_ref, alog_ref, dtb_ref, gw_ref,
                       o_ref, s_out_ref, ext_ref, ab_scr):
    dk = GDN_HEAD_DIM
    n_hist = CONV_WIDTH - 1
    n_seq = u_ref.shape[0]
    tile = (SUBLANES, LANES)
    tail = jnp.zeros((LANES - SUBLANES, LANES), F32)
    pad_rows = lambda x: jnp.concatenate([x, tail], axis=0)
    row = lax.broadcasted_iota(jnp.int32, tile, 0)
    lane = lax.broadcasted_iota(jnp.int32, tile, 1)
    real = row < t_new
    ext_ref[...] = jnp.zeros_like(ext_ref)
    ab_scr[...] = jnp.zeros_like(ab_scr)

    problems = [(g, h) for g in range(n_seq) for h in range(GDN_HEADS)]
    q8, k8, v8, e_col, b_col, w_col, decay, e_last = {}, {}, {}, {}, {}, {}, {}, {}
    for g in range(n_seq):
        ext_ref[g, 0:n_hist, :] = hist_ref[g]
        ext_ref[g, n_hist:n_hist + t_new, :] = u_ref[g]
        ab_scr[g, 0:t_new, :] = ab_ref[g]
        conv = _conv_silu(ext_ref.at[g], 0, SUBLANES, cw_ref)
        g_all, beta_all = _gate_terms(ab_scr[g], alog_ref, dtb_ref)
        cg = g_all
        for shift in range(1, t_new):
            cg = cg + jnp.where(row >= shift, pltpu.roll(g_all, shift, axis=0), 0.0)
        cg_last = cg[t_new - 1:t_new, :]
        e_all = jnp.exp(cg)
        w_all = jnp.exp(cg_last - cg)
        cg_t = pad_rows(cg).T
        for h in range(GDN_HEADS):
            p = (g, h)
            q8[p] = _l2norm(conv[:, h * dk:(h + 1) * dk]) * (dk ** -0.5)
            k8[p] = _l2norm(conv[:, GDN_DIM + h * dk:GDN_DIM + (h + 1) * dk])
            v8[p] = conv[:, 2 * GDN_DIM + h * dk:2 * GDN_DIM + (h + 1) * dk]
            e_col[p] = e_all[:, h:h + 1]
            b_col[p] = beta_all[:, GDN_HEADS + h:GDN_HEADS + h + 1]
            w_col[p] = w_all[:, h:h + 1]
            e_last[p] = e_all[t_new - 1:t_new, h:h + 1]
            decay[p] = jnp.exp(jnp.where(lane <= row, cg[:, h:h + 1] - cg_t[h:h + 1, :], -jnp.inf))

    states = {p: s_in_ref[p[0], p[1]] for p in problems}
    kq = {p: jnp.concatenate([k8[p], q8[p]], axis=0).astype(BF16) for p in problems}
    ks0_qs0 = {p: _dot(kq[p], states[p].astype(BF16)) for p in problems}
    gram = {p: _dot_nt(kq[p], pad_rows(k8[p]).astype(BF16)) for p in problems}
    deltas = {}
    for p in problems:
        m = b_col[p] * jnp.where(lane < row, gram[p][:SUBLANES] * decay[p], 0.0)
        r = b_col[p] * (v8[p] - e_col[p] * ks0_qs0[p][:SUBLANES])
        for j in range(t_new - 1):
            r = r - m[:, j:j + 1] * r[j:j + 1, :]
        deltas[p] = jnp.where(real, r, 0.0)
    updates = {}
    for p in problems:
        kw_t = pad_rows(jnp.where(real, k8[p] * w_col[p], 0.0)).T
        updates[p] = _dot(kw_t.astype(BF16), pad_rows(deltas[p]).astype(BF16))
    for g, h in problems:
        p = (g, h)
        cols = slice(h * dk, (h + 1) * dk)
        coef = gram[p][SUBLANES:] * decay[p]
        o = e_col[p] * ks0_qs0[p][SUBLANES:]
        for j in range(t_new):
            o = o + coef[:, j:j + 1] * deltas[p][j:j + 1, :]
        o = _gated_out_norm(o[:t_new], z_ref[g, :, cols].astype(F32), gw_ref[...])
        o_ref[g, :, cols] = o.astype(BF16)
        s_out_ref[g, h] = states[p] * e_last[p] + updates[p]


def _gdn_decode(u, hist, ab, z, s_in, conv_w, alog_pad, dtb_pad, gw):
    n_seq, t_new, _ = u.shape
    grp = DEC_GROUP
    seq = lambda r, width: pl.BlockSpec((grp, r, width), lambda i: (i, 0, 0))
    state = pl.BlockSpec((grp, GDN_HEADS, GDN_HEAD_DIM, GDN_HEAD_DIM), lambda i: (i, 0, 0, 0))
    return pl.pallas_call(
        functools.partial(_gdn_decode_kernel, t_new),
        grid=(n_seq // grp,),
        in_specs=[seq(t_new, CONV_DIM), seq(CONV_WIDTH - 1, CONV_DIM), seq(t_new, LANES), seq(t_new, GDN_DIM),
                  state, _const_spec((CONV_WIDTH, CONV_DIM)), _const_spec((1, LANES)), _const_spec((1, LANES)),
                  _const_spec((1, GDN_HEAD_DIM))],
        out_specs=(seq(t_new, GDN_DIM), state),
        out_shape=(jax.ShapeDtypeStruct((n_seq, t_new, GDN_DIM), BF16),
                   jax.ShapeDtypeStruct(s_in.shape, F32)),
        scratch_shapes=[pltpu.VMEM((grp, 2 * SUBLANES, CONV_DIM), F32), pltpu.VMEM((grp, SUBLANES, LANES), F32)],
        compiler_params=_params("parallel"),
        name="gdn_decode",
    )(u, hist, ab, z, s_in, conv_w, alog_pad, dtb_pad, gw)


def _tail_kernel(x_ref, oa_ref, ob_ref, sg_ref, wpa_ref, wpb_ref, wo_ref, nw_ref, wup_ref, wdown_ref, nf_ref,
                 y_ref, acc_ref, hm_ref):
    f = pl.program_id(1)

    @pl.when(f == 0)
    def _():
        for r0 in range(0, x_ref.shape[0], MERGE_ROWS):
            r = slice(r0, min(r0 + MERGE_ROWS, x_ref.shape[0]))
            y_a = _dot(oa_ref[r, :], wpa_ref[...])
            y_b = _dot(ob_ref[r, :], wpb_ref[...])
            mix = sg_ref[r, :D_MODEL] * y_a + sg_ref[r, D_MODEL:] * y_b
            x1 = x_ref[r, :] + _dot(mix.astype(BF16), wo_ref[...])
            acc_ref[r, :] = x1
            hm_ref[r, :] = _rmsnorm(x1, nw_ref[...]).astype(BF16)

    up = jnp.maximum(_dot(hm_ref[...], wup_ref[...]), 0.0)
    acc_ref[...] += _dot((up * up).astype(BF16), wdown_ref[...])

    @pl.when(f == pl.num_programs(1) - 1)
    def _():
        y_ref[...] = _rmsnorm(acc_ref[...], nf_ref[...])


def _tail(x2d, o_a, o_b, sg, lw, norm_f, tm, tf):
    n = x2d.shape[0]
    row = lambda width: pl.BlockSpec((tm, width), lambda i, f: (i, 0))
    return pl.pallas_call(
        _tail_kernel,
        grid=(n // tm, D_FF // tf),
        in_specs=[row(D_MODEL), row(SB_DIM), row(GDN_DIM), row(2 * D_MODEL),
                  _const_spec(lw["w_pa"].shape), _const_spec(lw["w_pb"].shape), _const_spec(lw["w_o"].shape),
                  _const_spec((1, D_MODEL)),
                  pl.BlockSpec((D_MODEL, tf), lambda i, f: (0, f)),
                  pl.BlockSpec((tf, D_MODEL), lambda i, f: (f, 0)),
                  _const_spec((1, D_MODEL))],
        out_specs=row(D_MODEL),
        out_shape=jax.ShapeDtypeStruct((n, D_MODEL), F32),
        scratch_shapes=[pltpu.VMEM((tm, D_MODEL), F32), pltpu.VMEM((tm, D_MODEL), BF16)],
        compiler_params=_params("parallel", "arbitrary"),
        name="tail",
    )(x2d, o_a, o_b, sg, lw["w_pa"], lw["w_pb"], lw["w_o"], lw["norm_mlp_w"], lw["w_up"], lw["w_down"], norm_f)


def _permute_w_in(w):
    n_main = 3 * SB_DIM + CONV_DIM + GDN_DIM
    gates = w[:, n_main + 2 * GDN_HEADS:]
    ab = w[:, n_main:n_main + 2 * GDN_HEADS]
    pad = jnp.zeros((w.shape[0], LANES - 2 * GDN_HEADS), w.dtype)
    return jnp.concatenate([w[:, :n_main], gates, ab, pad], axis=1).astype(BF16)


def _lane_pad(vec, offset):
    return jnp.zeros((1, LANES), F32).at[0, offset:offset + vec.shape[0]].set(vec.astype(F32))


def _layer_common(x2d, lw, tm, page=None):
    return _inproj(x2d, lw["norm_mix_w"], lw["w_in"], tm, page)


def kernel(x_prompt, x_sample, cache_k, cache_v, page_table, state_conv, state_ssm, norm_mix_w, w_in, sb_bias,
           conv_w, a_log, dt_bias, gdn_norm_w, w_pa, w_pb, w_o, norm_mlp_w, w_up, w_down, norm_final_w):
    depth = w_in.shape[0]
    assert depth == 1, "the residual stream is normalised once, after the only layer"
    b_p, t_p, _ = x_prompt.shape
    b_s, t_s, _ = x_sample.shape
    n_pool, page = cache_k.shape[1], cache_k.shape[2]
    l = 0
    lw = {
        "norm_mix_w": norm_mix_w[l].reshape(1, D_MODEL).astype(F32),
        "w_in": _permute_w_in(w_in[l]),
        "w_pa": w_pa[l].astype(BF16), "w_pb": w_pb[l].astype(BF16), "w_o": w_o[l].astype(BF16),
        "norm_mlp_w": norm_mlp_w[l].reshape(1, D_MODEL).astype(F32),
        "w_up": w_up[l].astype(BF16), "w_down": w_down[l].astype(BF16),
    }
    norm_f = norm_final_w.reshape(1, D_MODEL).astype(F32)
    bias = sb_bias[l].astype(F32)
    cw = conv_w[l].astype(F32)
    alog_pad = _lane_pad(a_log[l], 0)
    dtb_pad = _lane_pad(dt_bias[l], 0)
    gw = gdn_norm_w[l].reshape(1, GDN_HEAD_DIM).astype(F32)
    ntri = -jnp.tril(jnp.ones((SB_BLOCK, SB_BLOCK), BF16), -1)
    lincl = jnp.tril(jnp.ones((GDN_CHUNK, GDN_CHUNK), BF16))

    xp = x_prompt.reshape(b_p * t_p, D_MODEL).astype(F32)
    q, k_t, v_t, kb, vb, u, z, ab, sg = _layer_common(xp, lw, 256, page)
    shp = lambda a: a.reshape(b_p, t_p, a.shape[-1])
    o_a = _sb_prompt(bias, shp(q), shp(kb), shp(vb), ntri)
    un, wn, qg, kg, att, cg = _gdn_prep(shp(u), shp(ab), cw, alog_pad, dtb_pad, lincl)
    o_b, ssm_p = _gdn_scan(un, wn, qg, kg, att, cg, shp(z), gw)
    y_p = _tail(xp, o_a.reshape(-1, SB_DIM), o_b.reshape(-1, GDN_DIM), sg, lw, norm_f, 1024, 512)
    y_prompt = y_p.reshape(b_p, t_p, D_MODEL).astype(x_prompt.dtype)
    as_pages = lambda a_t: jnp.transpose(
        a_t.reshape(b_p, t_p // page, SB_HEADS, SB_HEAD_DIM, page), (0, 1, 4, 2, 3))[None]
    new_k_prompt = as_pages(k_t).astype(cache_k.dtype)
    new_v_prompt = as_pages(v_t).astype(cache_v.dtype)
    new_conv_prompt = shp(u)[None, :, t_p - (CONV_WIDTH - 1):, :].astype(state_conv.dtype)
    new_ssm_prompt = ssm_p[None].astype(state_ssm.dtype)

    xs = x_sample.reshape(b_s * t_s, D_MODEL).astype(F32)
    tm_s = min(256, b_s * t_s)
    q, k, v, _, _, u, z, ab, sg = _layer_common(xs, lw, tm_s)
    shs = lambda a: a.reshape(b_s, t_s, a.shape[-1])
    pad8 = lambda a: jnp.pad(shs(a), ((0, 0), (0, SUBLANES - t_s), (0, 0)))
    pages_t = lambda c: jnp.transpose(c[l], (0, 2, 3, 1)).reshape(n_pool, SB_DIM, page).astype(F32)
    o_a = _sb_decode(page_table, bias, shs(q), pad8(k), pad8(v), ntri, pages_t(cache_k), pages_t(cache_v))
    o_b, ssm_s = _gdn_decode(shs(u), state_conv[l].astype(F32), shs(ab), shs(z), state_ssm[l].astype(F32),
                             cw, alog_pad, dtb_pad, gw)
    y_s = _tail(xs, o_a.reshape(-1, SB_DIM), o_b.reshape(-1, GDN_DIM), sg, lw, norm_f, tm_s, 512)
    y_sample = y_s.reshape(b_s, t_s, D_MODEL).astype(x_sample.dtype)
    head_shape = (depth, b_s, t_s, SB_HEADS, SB_HEAD_DIM)
    new_k_sample = k.reshape(head_shape).astype(cache_k.dtype)
    new_v_sample = v.reshape(head_shape).astype(cache_v.dtype)
    new_conv_sample = jnp.concatenate([state_conv[l].astype(F32), shs(u)], axis=1)[None, :, t_s:, :].astype(state_conv.dtype)
    new_ssm_sample = ssm_s[None].astype(state_ssm.dtype)

    return (y_prompt, y_sample, new_k_prompt, new_v_prompt, new_k_sample, new_v_sample,
            new_conv_prompt, new_conv_sample, new_ssm_prompt, new_ssm_sample)
```

```python
import functools
import math

import jax
import jax.numpy as jnp
from jax import lax
from jax.experimental import pallas as pl
from jax.experimental.pallas import tpu as pltpu

F32 = jnp.float32
BF16 = jnp.bfloat16

D_MODEL = 1024
SB_HEADS = 8
SB_HEAD_DIM = 64
SB_DIM = SB_HEADS * SB_HEAD_DIM
GDN_HEADS = 4
GDN_HEAD_DIM = 128
GDN_DIM = GDN_HEADS * GDN_HEAD_DIM
CONV_WIDTH = 4
CONV_DIM = 3 * GDN_DIM
GDN_CHUNK = 64
D_FF = 4 * D_MODEL
NORM_EPS = 1e-6
L2_EPS = 1e-6
NEG_LOG2E = -1.4426950408889634

LANES = 128
SUBLANES = 8
VMEM_LIMIT_BYTES = 56 * 1024 * 1024

COL_Q = 0
COL_K = COL_Q + SB_DIM
COL_V = COL_K + SB_DIM
COL_U = COL_V + SB_DIM
COL_Z = COL_U + CONV_DIM
COL_GA = COL_Z + GDN_DIM
COL_GB = COL_GA + D_MODEL
COL_AB = COL_GB + D_MODEL
IN_COLS = COL_AB + LANES

SB_BLOCK = 256
SB_ROW_CHUNK = 64
PREP_ROWS = 512
SCAN_ROWS = 128
MERGE_ROWS = 512
DEC_GROUP = 8


def _dot(a, b):
    return jnp.dot(a, b, preferred_element_type=F32)


def _dot_nt(a, b):
    return lax.dot_general(a, b, (((1,), (1,)), ((), ())), preferred_element_type=F32)


def _dot_tn(a, b):
    return lax.dot_general(a, b, (((0,), (0,)), ((), ())), preferred_element_type=F32)


def _dot_exact01(m01, x):
    h = x.astype(BF16)
    r = x - h.astype(F32)
    m = r.astype(BF16)
    l = (r - m.astype(F32)).astype(BF16)
    return _dot(m01, h) + (_dot(m01, m) + _dot(m01, l))


def _softplus(z):
    return jnp.maximum(z, 0.0) + jnp.log(1.0 + jnp.exp2(jnp.abs(z) * NEG_LOG2E))


def _sigmoid(z):
    return 1.0 / (1.0 + jnp.exp(-z))


def _silu(z):
    return z * _sigmoid(z)


def _rmsnorm(x, w):
    return x * lax.rsqrt(jnp.mean(x * x, axis=-1, keepdims=True) + NORM_EPS) * w


def _const_spec(shape):
    nd = len(shape)
    return pl.BlockSpec(shape, lambda *_: (0,) * nd, pipeline_mode=pl.Buffered(1))


def _params(*sem):
    return pltpu.CompilerParams(dimension_semantics=sem, vmem_limit_bytes=VMEM_LIMIT_BYTES)


def _inproj_kernel(page, tiles_per_seq, x_ref, nw_ref, w_ref, *refs):
    if tiles_per_seq is None:
        q_ref, k_ref, v_ref, kb_ref, vb_ref, u_ref, z_ref, ab_ref, sg_ref = refs
    else:
        cw_ref, q_ref, k_ref, v_ref, kb_ref, vb_ref, u_ref, z_ref, ab_ref, sg_ref, tail_ref, ext_ref = refs
    x = x_ref[...]
    hb = _rmsnorm(x, nw_ref[...]).astype(BF16)

    def seg(lo, width):
        return _dot(hb, w_ref[:, lo:lo + width])

    def store_kv(ref, val):
        if page is None:
            ref[...] = val
        else:
            val_t = val.T
            for p in range(val.shape[0] // page):
                ref[p] = val_t[:, p * page:(p + 1) * page]

    fused_conv = tiles_per_seq is not None
    u_cols = [slice(j * SB_DIM, (j + 1) * SB_DIM) for j in range(CONV_DIM // SB_DIM)]
    if fused_conv:
        tm = x.shape[0]

        @pl.when(pl.program_id(0) % tiles_per_seq == 0)
        def _():
            ext_ref[0:SUBLANES, :] = jnp.zeros((SUBLANES, CONV_DIM), F32)

        for j, cols in enumerate(u_cols):
            ext_ref[SUBLANES:, cols] = seg(COL_U + j * SB_DIM, SB_DIM)
    else:
        for j, cols in enumerate(u_cols):
            u_ref[:, cols] = seg(COL_U + j * SB_DIM, SB_DIM)

    def conv_chunk(j):
        if fused_conv:
            u_ref[:, u_cols[j]] = _conv_silu(ext_ref, SUBLANES - (CONV_WIDTH - 1), tm, cw_ref, u_cols[j])

    conv_chunk(0)
    q_ref[...] = (seg(COL_Q, SB_DIM) * (SB_HEAD_DIM ** -0.5)).astype(BF16)
    conv_chunk(1)
    k = seg(COL_K, SB_DIM)
    store_kv(k_ref, k)
    kb_ref[...] = k.astype(BF16)
    conv_chunk(2)
    v = seg(COL_V, SB_DIM)
    store_kv(v_ref, v)
    vb_ref[...] = v.astype(BF16)
    if fused_conv:
        last_rows = ext_ref[tm:tm + SUBLANES, :]
        tail_ref[0] = last_rows
        ext_ref[0:SUBLANES, :] = last_rows
    z_ref[...] = seg(COL_Z, GDN_DIM).astype(BF16)
    for j in range(2 * D_MODEL // SB_DIM):
        sg_ref[:, j * SB_DIM:(j + 1) * SB_DIM] = _sigmoid(seg(COL_GA + j * SB_DIM, SB_DIM)).astype(BF16)
    ab_ref[...] = seg(COL_AB, LANES)


def _inproj(x2d, norm_w, w_perm, tm, page=None, conv_w=None, seq_len=None):
    n = x2d.shape[0]
    row = lambda width: pl.BlockSpec((tm, width), lambda i: (i, 0))
    fused_conv = conv_w is not None
    if page is None:
        kv_shape, kv_spec = jax.ShapeDtypeStruct((n, SB_DIM), F32), row(SB_DIM)
    else:
        kv_shape = jax.ShapeDtypeStruct((n // page, SB_DIM, page), F32)
        kv_spec = pl.BlockSpec((tm // page, SB_DIM, page), lambda i: (i, 0, 0))
    out_shapes = (
        jax.ShapeDtypeStruct((n, SB_DIM), BF16),
        kv_shape,
        kv_shape,
        jax.ShapeDtypeStruct((n, SB_DIM), BF16),
        jax.ShapeDtypeStruct((n, SB_DIM), BF16),
        jax.ShapeDtypeStruct((n, CONV_DIM), F32),
        jax.ShapeDtypeStruct((n, GDN_DIM), BF16),
        jax.ShapeDtypeStruct((n, LANES), F32),
        jax.ShapeDtypeStruct((n, 2 * D_MODEL), BF16),
    )
    out_specs = tuple(kv_spec if i in (1, 2) else row(s.shape[1]) for i, s in enumerate(out_shapes))
    in_specs = [row(D_MODEL), _const_spec((1, D_MODEL)), _const_spec((D_MODEL, IN_COLS))]
    operands = (x2d, norm_w, w_perm)
    scratch = []
    if fused_conv:
        in_specs.append(_const_spec((CONV_WIDTH, CONV_DIM)))
        operands += (conv_w,)
        out_shapes += (jax.ShapeDtypeStruct((n // tm, SUBLANES, CONV_DIM), F32),)
        out_specs += (pl.BlockSpec((1, SUBLANES, CONV_DIM), lambda i: (i, 0, 0)),)
        scratch = [pltpu.VMEM((tm + SUBLANES, CONV_DIM), F32)]
    return pl.pallas_call(
        functools.partial(_inproj_kernel, page, seq_len // tm if fused_conv else None),
        grid=(n // tm,),
        in_specs=in_specs,
        out_specs=out_specs,
        out_shape=out_shapes,
        scratch_shapes=scratch,
        compiler_params=_params("arbitrary" if fused_conv else "parallel"),
        name="inproj",
    )(*operands)


def _sb_tile(z, carry, ntri, mask):
    sp = _softplus(z)
    if mask is not None:
        sp = jnp.where(mask, sp, 0.0)
    suffix = _dot(sp.astype(BF16), ntri)
    w = jnp.exp(z - sp + suffix + carry)
    if mask is not None:
        w = jnp.where(mask, w, 0.0)
    return w.astype(BF16), carry - jnp.sum(sp, axis=1, keepdims=True)


def _sb_prompt_kernel(bias_ref, q_ref, k_ref, v_ref, ntri_ref, o_ref,
                      qs_scr, bias_scr, z_scr, beta_scr, decay_scr, run_scr, acc_scr):
    hp = pl.program_id(1)
    tb = SB_BLOCK
    n_q = q_ref.shape[1] // tb
    n_tiles = n_q * (n_q + 1) // 2
    lane = lax.broadcasted_iota(jnp.int32, (1, LANES), 1)
    head_lanes = [lane < SB_HEAD_DIM, lane >= SB_HEAD_DIM]
    zero = jnp.zeros((), BF16)
    rows = [slice(j * tb, (j + 1) * tb) for j in range(2)]

    for ref in (z_scr, beta_scr, decay_scr, run_scr, acc_scr):
        ref[...] = jnp.zeros_like(ref)
    row = lax.broadcasted_iota(jnp.int32, (tb, tb), 0)
    col = lax.broadcasted_iota(jnp.int32, (tb, tb), 1)
    for j in range(2):
        bias = bias_ref[2 * hp + j]
        bias_scr[0, j] = jnp.full((tb, tb), bias, F32)
        bias_scr[1, j] = jnp.where(col < row, bias, -1e30)
    for i in range(n_q):
        q = q_ref[0, i * tb:(i + 1) * tb, :]
        for j in range(2):
            qs_scr[i, rows[j], :] = jnp.where(head_lanes[j], q, zero)

    def key_block(ref, kb):
        return ref[0, pl.ds(pl.multiple_of(kb * tb, tb), tb), :]

    def step(s, tiles):
        (q_a, k_a), (q_b, k_b), (q_d, k_d) = tiles
        chunks = [slice(c * SB_ROW_CHUNK, (c + 1) * SB_ROW_CHUNK) for c in range(2 * tb // SB_ROW_CHUNK)]
        zs = [z_scr[r, :] for r in chunks]
        z_scr[...] = _dot_nt(qs_scr[q_a], key_block(k_ref, k_a))
        valid_d = jnp.logical_and(s >= 2, s - 2 < n_tiles)
        v = key_block(v_ref, k_d)
        pv = None
        for j in range(2):
            vm = jnp.where(jnp.logical_and(head_lanes[j], valid_d), v, zero)
            w = beta_scr[rows[j], :] * decay_scr[rows[j], :]
            term = _dot(w, vm)
            pv = term if pv is None else pv + term
        acc_rows = pl.ds(pl.multiple_of(q_d * tb, tb), tb)
        acc_scr[acc_rows, :] = acc_scr[acc_rows, :] + pv
        first_of_block = k_b == q_b
        offset = first_of_block.astype(jnp.int32)
        one = jnp.ones((), BF16)
        sps = []
        for r, z_raw in zip(chunks, zs):
            j, r_in_head = divmod(r.start, tb)
            z = z_raw + bias_scr[offset, j, r_in_head:r_in_head + SB_ROW_CHUNK, :]
            t = jnp.exp2(jnp.abs(z) * NEG_LOG2E)
            log1p_t = jnp.log(1.0 + t).astype(BF16)
            z16, t16 = z.astype(BF16), t.astype(BF16)
            sps.append(jnp.maximum(z16, zero) + log1p_t)
            beta_scr[r, :] = jnp.where(z16 >= zero, one, t16) / (one + t16)
        suffix = _dot(jnp.concatenate(sps, axis=0), ntri_ref[...])
        for r, sp in zip(chunks, sps):
            run = jnp.where(first_of_block, 0.0, run_scr[r, :])
            decay_scr[r, :] = jnp.exp(suffix[r, :] + run).astype(BF16)
            run_scr[r, :] = run + suffix[r, 0:1] - sp[:, 0:1].astype(F32)
        wrap = k_a == 0
        q_n = jnp.where(wrap, jnp.minimum(q_a + 1, n_q - 1), q_a)
        k_n = jnp.where(wrap, q_n, k_a - 1)
        return ((q_n, k_n),) + tiles[:2]

    first = (jnp.int32(0), jnp.int32(0))
    lax.fori_loop(0, n_tiles + 2, step, (first,) * 3)
    o_ref[0] = acc_scr[...].astype(BF16)


def _sb_prompt(sb_bias, q, k, v, ntri):
    bsz, t_len, _ = q.shape
    tb = SB_BLOCK
    seq = pl.BlockSpec((1, t_len, LANES), lambda b, h, *_: (b, 0, h))
    return pl.pallas_call(
        _sb_prompt_kernel,
        grid_spec=pltpu.PrefetchScalarGridSpec(
            num_scalar_prefetch=1,
            grid=(bsz, SB_DIM // LANES),
            in_specs=[seq, seq, seq, pl.BlockSpec((tb, tb), lambda b, h, *_: (0, 0))],
            out_specs=seq,
            scratch_shapes=[
                pltpu.VMEM((t_len // tb, 2 * tb, LANES), BF16),
                pltpu.VMEM((2, 2, tb, tb), F32),
                pltpu.VMEM((2 * tb, tb), F32),
                pltpu.VMEM((2 * tb, tb), BF16),
                pltpu.VMEM((2 * tb, tb), BF16),
                pltpu.VMEM((2 * tb, 1), F32),
                pltpu.VMEM((t_len, LANES), F32),
            ],
        ),
        out_shape=jax.ShapeDtypeStruct((bsz, t_len, SB_DIM), BF16),
        compiler_params=_params("parallel", "parallel"),
        name="sb_prompt",
    )(sb_bias, q, k, v, ntri)


def _sb_decode_kernel(n_pages, t_new, pt_ref, bias_ref, q_ref, kn_ref, vn_ref, ntri_ref, *refs):
    kt_pages = refs[:n_pages]
    vt_pages = refs[n_pages:2 * n_pages]
    o_ref = refs[2 * n_pages]
    page = kt_pages[0].shape[2]
    rows = t_new * SB_HEADS
    r_iota = lax.broadcasted_iota(jnp.int32, (rows, SB_DIM), 0)
    l_iota = lax.broadcasted_iota(jnp.int32, (rows, SB_DIM), 1)
    head_of_row = r_iota % SB_HEADS
    own_lanes = (l_iota // SB_HEAD_DIM) == head_of_row

    q = q_ref[0].astype(F32)
    q_rep = jnp.broadcast_to(q[:, None, :], (t_new, SB_HEADS, SB_DIM)).reshape(rows, SB_DIM)
    q_bd = jnp.where(own_lanes, q_rep, 0.0).astype(BF16)

    r1 = lax.broadcasted_iota(jnp.int32, (rows, 1), 0)
    bias = jnp.zeros((rows, 1), F32)
    for h in range(SB_HEADS):
        bias = jnp.where(r1 % SB_HEADS == h, bias_ref[h], bias)

    pad = jnp.zeros((page - SUBLANES, SB_DIM), F32)
    k_new = jnp.concatenate([kn_ref[0], pad], axis=0).astype(BF16)
    v_new = jnp.concatenate([vn_ref[0], pad], axis=0).astype(BF16)
    kcol = lax.broadcasted_iota(jnp.int32, (rows, page), 1)
    qtok = lax.broadcasted_iota(jnp.int32, (rows, page), 0) // SB_HEADS
    z = _dot_nt(q_bd, k_new) + bias
    w, carry = _sb_tile(z, jnp.zeros((rows, 1), F32), ntri_ref[:page, :page], kcol < qtok)
    acc = _dot(w, v_new)

    blk = ntri_ref.shape[0] // page
    order = range(n_pages // blk - 1, -1, -1)
    lane_cat = lambda pages, pb: jnp.concatenate(
        [pages[pb * blk + i][0] for i in range(blk)], axis=1).astype(BF16)
    zs = [_dot(q_bd, lane_cat(kt_pages, pb)) + bias for pb in order]
    sps = [_softplus(z) for z in zs]
    suffixes = [_dot(sp.astype(BF16), ntri_ref[...]) for sp in sps]
    for pb, z, sp, suffix in zip(order, zs, sps, suffixes):
        w = jnp.exp(z - sp + suffix + carry).astype(BF16)
        carry = carry - jnp.sum(sp, axis=1, keepdims=True)
        acc = acc + _dot_nt(w, lane_cat(vt_pages, pb))

    acc = jnp.where(own_lanes, acc, 0.0)
    o_ref[0] = jnp.sum(acc.reshape(t_new, SB_HEADS, SB_DIM), axis=1).astype(BF16)


def _sb_decode(page_table, sb_bias, q, k_new8, v_new8, ntri, cache_kt, cache_vt):
    n_seq, t_new, _ = q.shape
    n_pages = page_table.shape[1]
    page = cache_kt.shape[2]

    def page_spec(p):
        return pl.BlockSpec((1, SB_DIM, page), lambda s, pt, b: (pt[s, p], 0, 0))

    seq_spec = lambda r: pl.BlockSpec((1, r, SB_DIM), lambda s, pt, b: (s, 0, 0))
    return pl.pallas_call(
        functools.partial(_sb_decode_kernel, n_pages, t_new),
        grid_spec=pltpu.PrefetchScalarGridSpec(
            num_scalar_prefetch=2,
            grid=(n_seq,),
            in_specs=[seq_spec(t_new), seq_spec(SUBLANES), seq_spec(SUBLANES),
                      pl.BlockSpec(ntri.shape, lambda s, pt, b: (0, 0))]
                     + [page_spec(p) for p in range(n_pages)] * 2,
            out_specs=seq_spec(t_new),
        ),
        out_shape=jax.ShapeDtypeStruct((n_seq, t_new, SB_DIM), BF16),
        compiler_params=_params("parallel"),
        name="sb_decode",
    )(page_table, sb_bias, q, k_new8, v_new8, ntri, *([cache_kt] * n_pages), *([cache_vt] * n_pages))


def _conv_silu(ext_ref, first, rows, cw_ref, cols=slice(None)):
    acc = None
    for w in range(CONV_WIDTH):
        term = ext_ref[first + w:first + w + rows, cols] * cw_ref[w:w + 1, cols]
        acc = term if acc is None else acc + term
    return _silu(acc)


def _l2norm(x):
    return x * lax.rsqrt(jnp.sum(x * x, axis=-1, keepdims=True) + L2_EPS)


def _gate_terms(ab, alog_ref, dtb_ref):
    g = -jnp.exp(alog_ref[...]) * _softplus(ab + dtb_ref[...])
    return g, _sigmoid(ab)


def _gated_out_norm(o, z, gw):
    o = o * lax.rsqrt(jnp.mean(o * o, axis=-1, keepdims=True) + NORM_EPS) * gw
    return o * _silu(z)


def _gdn_prep_kernel(conv_ref, ab_ref, alog_ref, dtb_ref, lincl_ref, hones_ref,
                     un_ref, wn_ref, qg_ref, kg_ref, att_ref, cg_ref):
    rows = conv_ref.shape[1]
    c_len = GDN_CHUNK
    dk = GDN_HEAD_DIM
    conv = conv_ref[0]

    g_all, beta_all = _gate_terms(ab_ref[0], alog_ref, dtb_ref)
    r_i = lax.broadcasted_iota(jnp.int32, (c_len, c_len), 0)
    c_i = lax.broadcasted_iota(jnp.int32, (c_len, c_len), 1)
    incl = c_i <= r_i
    strict = c_i < r_i
    lincl = lincl_ref[...]
    n_chunks = rows // c_len

    cg_chunks = [_dot_exact01(lincl, g_all[c * c_len:(c + 1) * c_len, :]) for c in range(n_chunks)]
    cg_t_pairs = []
    for pair in range(n_chunks // 2):
        cg_pair = jnp.concatenate(cg_chunks[2 * pair:2 * pair + 2], axis=0)
        cg_ref[0, 2 * pair * c_len:(2 * pair + 2) * c_len, :] = cg_pair
        cg_t_pairs.append(cg_pair.T)

    problems = [(c, h) for c in range(n_chunks) for h in range(GDN_HEADS)]

    def head_cols(base, h):
        return slice(base + h * dk, base + (h + 1) * dk)

    def l2norm_heads(x):
        sq = x * x
        hi = sq.astype(BF16)
        lo = (sq - hi.astype(F32)).astype(BF16)
        return x * lax.rsqrt(_dot(hi, hones_ref[...]) + _dot(lo, hones_ref[...]) + L2_EPS)

    q_all = l2norm_heads(conv[:, :GDN_DIM]) * (dk ** -0.5)
    k_all = l2norm_heads(conv[:, GDN_DIM:2 * GDN_DIM])

    qs, ks, vs, bcs, cgcols, decays = [], [], [], [], [], []
    for c, h in problems:
        r = slice(c * c_len, (c + 1) * c_len)
        qs.append(q_all[r, head_cols(0, h)])
        ks.append(k_all[r, head_cols(0, h)])
        vs.append(conv[r, head_cols(2 * GDN_DIM, h)])
        bcs.append(beta_all[r, GDN_HEADS + h:GDN_HEADS + h + 1])
        cg_col = cg_chunks[c][:, h:h + 1]
        cg_row = cg_t_pairs[c // 2][h:h + 1, (c % 2) * c_len:(c % 2 + 1) * c_len]
        cgcols.append(cg_col)
        decays.append(jnp.exp(jnp.where(incl, cg_col - cg_row, -jnp.inf)))
    kbs = [k.astype(BF16) for k in ks]
    kks = [_dot_nt(kb, kb) for kb in kbs]

    powers = [jnp.where(strict, -(bc * kk * dec), 0.0) for bc, kk, dec in zip(bcs, kks, decays)]
    inv_off = powers
    for _ in range(int(math.log2(c_len)) - 1):
        pbs = [p.astype(BF16) for p in powers]
        powers = [_dot(pb, pb) for pb in pbs]
        next_pbs = [p.astype(BF16) for p in powers]
        inv_off = [t + p + _dot(t.astype(BF16), pb) for t, p, pb in zip(inv_off, powers, next_pbs)]

    for (c, h), q, k, v, bc, cg_col, dec, t_off in zip(problems, qs, ks, vs, bcs, cgcols, decays, inv_off):
        r = slice(c * c_len, (c + 1) * c_len)
        cols = head_cols(0, h)
        rhs = jnp.concatenate([v * bc, k * (bc * jnp.exp(cg_col))], axis=1)
        sol = rhs + _dot(t_off.astype(BF16), rhs.astype(BF16))
        un_ref[0, r, cols] = sol[:, :dk]
        wn_ref[0, r, cols] = sol[:, dk:].astype(BF16)
        qg_ref[0, r, cols] = (q * jnp.exp(cg_col)).astype(BF16)
        g_last = cg_chunks[c][c_len - 1:c_len, h:h + 1]
        kg_ref[0, r, cols] = (k * jnp.exp(g_last - cg_col)).astype(BF16)
        qk = _dot_nt(q.astype(BF16), k.astype(BF16))
        att_ref[0, h, r, :] = (qk * dec).astype(BF16)


def _gdn_prep(conv, ab, alog_pad, dtb_pad, lincl):
    bsz, t_len, _ = conv.shape
    head_of = jnp.arange(GDN_DIM) // GDN_HEAD_DIM
    head_ones = (head_of[:, None] == head_of[None, :]).astype(BF16)
    rows = min(PREP_ROWS, t_len)
    blk = lambda width: pl.BlockSpec((1, rows, width), lambda b, i: (b, i, 0))
    out_shapes = (
        jax.ShapeDtypeStruct((bsz, t_len, GDN_DIM), F32),
        jax.ShapeDtypeStruct((bsz, t_len, GDN_DIM), BF16),
        jax.ShapeDtypeStruct((bsz, t_len, GDN_DIM), BF16),
        jax.ShapeDtypeStruct((bsz, t_len, GDN_DIM), BF16),
        jax.ShapeDtypeStruct((bsz, GDN_HEADS, t_len, GDN_CHUNK), BF16),
        jax.ShapeDtypeStruct((bsz, t_len, LANES), F32),
    )
    return pl.pallas_call(
        _gdn_prep_kernel,
        grid=(bsz, t_len // rows),
        in_specs=[
            blk(CONV_DIM),
            blk(LANES),
            _const_spec((1, LANES)),
            _const_spec((1, LANES)),
            _const_spec((GDN_CHUNK, GDN_CHUNK)),
            _const_spec((GDN_DIM, GDN_DIM)),
        ],
        out_specs=(blk(GDN_DIM), blk(GDN_DIM), blk(GDN_DIM), blk(GDN_DIM),
                   pl.BlockSpec((1, GDN_HEADS, rows, GDN_CHUNK), lambda b, i: (b, 0, i, 0)),
                   blk(LANES)),
        out_shape=out_shapes,
        compiler_params=_params("parallel", "parallel"),
        name="gdn_prep",
    )(conv, ab, alog_pad, dtb_pad, lincl, head_ones)


def _gdn_scan_kernel(un_ref, wn_ref, qg_ref, kg_ref, att_ref, cg_ref, z_ref, gw_ref, o_ref, s_out_ref, s_ref):
    c = pl.program_id(0)
    bsz = un_ref.shape[0]
    c_len = GDN_CHUNK
    dk = GDN_HEAD_DIM

    @pl.when(c == 0)
    def _():
        s_ref[...] = jnp.zeros_like(s_ref)

    chains = [(b, h, slice(h * dk, (h + 1) * dk)) for b in range(bsz) for h in range(GDN_HEADS)]
    states = [s_ref[b, h] for b, h, _ in chains]
    for sub in range(un_ref.shape[1] // c_len):
        r = slice(sub * c_len, (sub + 1) * c_len)
        decay_last = [jnp.exp(cg_ref[b, r.stop - 1:r.stop, :]) for b in range(bsz)]
        sbs = [s.astype(BF16) for s in states]
        ws = [_dot(wn_ref[b, r, cols], sb) for (b, _, cols), sb in zip(chains, sbs)]
        qs = [_dot(qg_ref[b, r, cols], sb) for (b, _, cols), sb in zip(chains, sbs)]
        vbs = [(un_ref[b, r, cols] - w).astype(BF16) for (b, _, cols), w in zip(chains, ws)]
        new_states = []
        for (b, h, cols), s, q_s, vb in zip(chains, states, qs, vbs):
            o = q_s + _dot(att_ref[b, h, r, :], vb)
            new_states.append(s * decay_last[b][:, h:h + 1] + _dot_tn(kg_ref[b, r, cols], vb))
            o_ref[b, r, cols] = _gated_out_norm(o, z_ref[b, r, cols].astype(F32), gw_ref[...]).astype(BF16)
        states = new_states
    for (b, h, _), s in zip(chains, states):
        s_ref[b, h] = s

    @pl.when(c == pl.num_programs(0) - 1)
    def _():
        s_out_ref[...] = s_ref[...]


def _gdn_scan(un, wn, qg, kg, att, cg, z, gw):
    bsz, t_len, _ = un.shape
    rows = min(SCAN_ROWS, t_len)
    blk = lambda width: pl.BlockSpec((bsz, rows, width), lambda c: (0, c, 0))
    state_shape = (bsz, GDN_HEADS, GDN_HEAD_DIM, GDN_HEAD_DIM)
    return pl.pallas_call(
        _gdn_scan_kernel,
        grid=(t_len // rows,),
        in_specs=[blk(GDN_DIM), blk(GDN_DIM), blk(GDN_DIM), blk(GDN_DIM),
                  pl.BlockSpec((bsz, GDN_HEADS, rows, GDN_CHUNK), lambda c: (0, 0, c, 0)),
                  blk(LANES), blk(GDN_DIM), _const_spec((1, GDN_HEAD_DIM))],
        out_specs=(blk(GDN_DIM), pl.BlockSpec(state_shape, lambda c: (0, 0, 0, 0))),
        out_shape=(jax.ShapeDtypeStruct((bsz, t_len, GDN_DIM), BF16),
                   jax.ShapeDtypeStruct(state_shape, F32)),
        scratch_shapes=[pltpu.VMEM(state_shape, F32)],
        compiler_params=_params("arbitrary"),
        name="gdn_scan",
    )(un, wn, qg, kg, att, cg, z, gw)


def _gdn_decode_kernel(t_new, u_ref, hist_ref, ab_ref, z_ref, s_in_ref, cw_ref, alog_ref, dtb_ref, gw_ref,
                       o_ref, s_out_ref, ext_ref, ab_scr):
    dk = GDN_HEAD_DIM
    n_hist = CONV_WIDTH - 1
    n_seq = u_ref.shape[0]
    tile = (SUBLANES, LANES)
    tail = jnp.zeros((LANES - SUBLANES, LANES), F32)
    pad_rows = lambda x: jnp.concatenate([x, tail], axis=0)
    row = lax.broadcasted_iota(jnp.int32, tile, 0)
    lane = lax.broadcasted_iota(jnp.int32, tile, 1)
    real = row < t_new
    ext_ref[...] = jnp.zeros_like(ext_ref)
    ab_scr[...] = jnp.zeros_like(ab_scr)

    problems = [(g, h) for g in range(n_seq) for h in range(GDN_HEADS)]
    q8, k8, v8, e_col, b_col, w_col, decay, e_last = {}, {}, {}, {}, {}, {}, {}, {}
    for g in range(n_seq):
        ext_ref[g, 0:n_hist, :] = hist_ref[g]
        ext_ref[g, n_hist:n_hist + t_new, :] = u_ref[g]
        ab_scr[g, 0:t_new, :] = ab_ref[g]
        conv = _conv_silu(ext_ref.at[g], 0, SUBLANES, cw_ref)
        g_all, beta_all = _gate_terms(ab_scr[g], alog_ref, dtb_ref)
        cg = g_all
        for shift in range(1, t_new):
            cg = cg + jnp.where(row >= shift, pltpu.roll(g_all, shift, axis=0), 0.0)
        cg_last = cg[t_new - 1:t_new, :]
        e_all = jnp.exp(cg)
        w_all = jnp.exp(cg_last - cg)
        cg_t = pad_rows(cg).T
        for h in range(GDN_HEADS):
            p = (g, h)
            q8[p] = _l2norm(conv[:, h * dk:(h + 1) * dk]) * (dk ** -0.5)
            k8[p] = _l2norm(conv[:, GDN_DIM + h * dk:GDN_DIM + (h + 1) * dk])
            v8[p] = conv[:, 2 * GDN_DIM + h * dk:2 * GDN_DIM + (h + 1) * dk]
            e_col[p] = e_all[:, h:h + 1]
            b_col[p] = beta_all[:, GDN_HEADS + h:GDN_HEADS + h + 1]
            w_col[p] = w_all[:, h:h + 1]
            e_last[p] = e_all[t_new - 1:t_new, h:h + 1]
            decay[p] = jnp.exp(jnp.where(lane <= row, cg[:, h:h + 1] - cg_t[h:h + 1, :], -jnp.inf))

    states = {p: s_in_ref[p[0], p[1]] for p in problems}
    kq = {p: jnp.concatenate([k8[p], q8[p]], axis=0).astype(BF16) for p in problems}
    ks0_qs0 = {p: _dot(kq[p], states[p].astype(BF16)) for p in problems}
    gram = {p: _dot_nt(kq[p], pad_rows(k8[p]).astype(BF16)) for p in problems}
    deltas = {}
    for p in problems:
        m = b_col[p] * jnp.where(lane < row, gram[p][:SUBLANES] * decay[p], 0.0)
        r = b_col[p] * (v8[p] - e_col[p] * ks0_qs0[p][:SUBLANES])
        for j in range(t_new - 1):
            r = r - m[:, j:j + 1] * r[j:j + 1, :]
        deltas[p] = jnp.where(real, r, 0.0)
    updates = {}
    for p in problems:
        kw_t = pad_rows(jnp.where(real, k8[p] * w_col[p], 0.0)).T
        updates[p] = _dot(kw_t.astype(BF16), pad_rows(deltas[p]).astype(BF16))
    for g, h in problems:
        p = (g, h)
        cols = slice(h * dk, (h + 1) * dk)
        coef = gram[p][SUBLANES:] * decay[p]
        o = e_col[p] * ks0_qs0[p][SUBLANES:]
        for j in range(t_new):
            o = o + coef[:, j:j + 1] * deltas[p][j:j + 1, :]
        o = _gated_out_norm(o[:t_new], z_ref[g, :, cols].astype(F32), gw_ref[...])
        o_ref[g, :, cols] = o.astype(BF16)
        s_out_ref[g, h] = states[p] * e_last[p] + updates[p]


def _gdn_decode(u, hist, ab, z, s_in, conv_w, alog_pad, dtb_pad, gw):
    n_seq, t_new, _ = u.shape
    grp = DEC_GROUP
    seq = lambda r, width: pl.BlockSpec((grp, r, width), lambda i: (i, 0, 0))
    state = pl.BlockSpec((grp, GDN_HEADS, GDN_HEAD_DIM, GDN_HEAD_DIM), lambda i: (i, 0, 0, 0))
    return pl.pallas_call(
        functools.partial(_gdn_decode_kernel, t_new),
        grid=(n_seq // grp,),
        in_specs=[seq(t_new, CONV_DIM), seq(CONV_WIDTH - 1, CONV_DIM), seq(t_new, LANES), seq(t_new, GDN_DIM),
                  state, _const_spec((CONV_WIDTH, CONV_DIM)), _const_spec((1, LANES)), _const_spec((1, LANES)),
                  _const_spec((1, GDN_HEAD_DIM))],
        out_specs=(seq(t_new, GDN_DIM), state),
        out_shape=(jax.ShapeDtypeStruct((n_seq, t_new, GDN_DIM), BF16),
                   jax.ShapeDtypeStruct(s_in.shape, F32)),
        scratch_shapes=[pltpu.VMEM((grp, 2 * SUBLANES, CONV_DIM), F32), pltpu.VMEM((grp, SUBLANES, LANES), F32)],
        compiler_params=_params("parallel"),
        name="gdn_decode",
    )(u, hist, ab, z, s_in, conv_w, alog_pad, dtb_pad, gw)


def _tail_kernel(x_ref, oa_ref, ob_ref, sg_ref, wpa_ref, wpb_ref, wo_ref, nw_ref, wup_ref, wdown_ref, nf_ref,
                 y_ref, acc_ref, hm_ref):
    f = pl.program_id(1)

    @pl.when(f == 0)
    def _():
        for r0 in range(0, x_ref.shape[0], MERGE_ROWS):
            r = slice(r0, min(r0 + MERGE_ROWS, x_ref.shape[0]))
            y_a = _dot(oa_ref[r, :], wpa_ref[...])
            y_b = _dot(ob_ref[r, :], wpb_ref[...])
            mix = sg_ref[r, :D_MODEL] * y_a + sg_ref[r, D_MODEL:] * y_b
            x1 = x_ref[r, :] + _dot(mix.astype(BF16), wo_ref[...])
            acc_ref[r, :] = x1
            hm_ref[r, :] = _rmsnorm(x1, nw_ref[...]).astype(BF16)

    up = jnp.maximum(_dot(hm_ref[...], wup_ref[...]), 0.0)
    acc_ref[...] += _dot((up * up).astype(BF16), wdown_ref[...])

    @pl.when(f == pl.num_programs(1) - 1)
    def _():
        y_ref[...] = _rmsnorm(acc_ref[...], nf_ref[...])


def _tail(x2d, o_a, o_b, sg, lw, norm_f, tm, tf):
    n = x2d.shape[0]
    row = lambda width: pl.BlockSpec((tm, width), lambda i, f: (i, 0))
    return pl.pallas_call(
        _tail_kernel,
        grid=(n // tm, D_FF // tf),
        in_specs=[row(D_MODEL), row(SB_DIM), row(GDN_DIM), row(2 * D_MODEL),
                  _const_spec(lw["w_pa"].shape), _const_spec(lw["w_pb"].shape), _const_spec(lw["w_o"].shape),
                  _const_spec((1, D_MODEL)),
                  pl.BlockSpec((D_MODEL, tf), lambda i, f: (0, f)),
                  pl.BlockSpec((tf, D_MODEL), lambda i, f: (f, 0)),
                  _const_spec((1, D_MODEL))],
        out_specs=row(D_MODEL),
        out_shape=jax.ShapeDtypeStruct((n, D_MODEL), F32),
        scratch_shapes=[pltpu.VMEM((tm, D_MODEL), F32), pltpu.VMEM((tm, D_MODEL), BF16)],
        compiler_params=_params("parallel", "arbitrary"),
        name="tail",
    )(x2d, o_a, o_b, sg, lw["w_pa"], lw["w_pb"], lw["w_o"], lw["norm_mlp_w"], lw["w_up"], lw["w_down"], norm_f)


def _permute_w_in(w):
    n_main = 3 * SB_DIM + CONV_DIM + GDN_DIM
    gates = w[:, n_main + 2 * GDN_HEADS:]
    ab = w[:, n_main:n_main + 2 * GDN_HEADS]
    pad = jnp.zeros((w.shape[0], LANES - 2 * GDN_HEADS), w.dtype)
    return jnp.concatenate([w[:, :n_main], gates, ab, pad], axis=1).astype(BF16)


def _lane_pad(vec, offset):
    return jnp.zeros((1, LANES), F32).at[0, offset:offset + vec.shape[0]].set(vec.astype(F32))


def _layer_common(x2d, lw, tm, **kwargs):
    return _inproj(x2d, lw["norm_mix_w"], lw["w_in"], tm, **kwargs)


def kernel(x_prompt, x_sample, cache_k, cache_v, page_table, state_conv, state_ssm, norm_mix_w, w_in, sb_bias,
           conv_w, a_log, dt_bias, gdn_norm_w, w_pa, w_pb, w_o, norm_mlp_w, w_up, w_down, norm_final_w):
    depth = w_in.shape[0]
    assert depth == 1, "the residual stream is normalised once, after the only layer"
    b_p, t_p, _ = x_prompt.shape
    b_s, t_s, _ = x_sample.shape
    n_pool, page = cache_k.shape[1], cache_k.shape[2]
    l = 0
    lw = {
        "norm_mix_w": norm_mix_w[l].reshape(1, D_MODEL).astype(F32),
        "w_in": _permute_w_in(w_in[l]),
        "w_pa": w_pa[l].astype(BF16), "w_pb": w_pb[l].astype(BF16), "w_o": w_o[l].astype(BF16),
        "norm_mlp_w": norm_mlp_w[l].reshape(1, D_MODEL).astype(F32),
        "w_up": w_up[l].astype(BF16), "w_down": w_down[l].astype(BF16),
    }
    norm_f = norm_final_w.reshape(1, D_MODEL).astype(F32)
    bias = sb_bias[l].astype(F32)
    cw = conv_w[l].astype(F32)
    alog_pad = _lane_pad(a_log[l], 0)
    dtb_pad = _lane_pad(dt_bias[l], 0)
    gw = gdn_norm_w[l].reshape(1, GDN_HEAD_DIM).astype(F32)
    ntri = -jnp.tril(jnp.ones((SB_BLOCK, SB_BLOCK), BF16), -1)
    lincl = jnp.tril(jnp.ones((GDN_CHUNK, GDN_CHUNK), BF16))

    xp = x_prompt.reshape(b_p * t_p, D_MODEL).astype(F32)
    tm_p = 256
    q, k_t, v_t, kb, vb, conv, z, ab, sg, u_tail = _layer_common(xp, lw, tm_p, page=page, conv_w=cw, seq_len=t_p)
    shp = lambda a: a.reshape(b_p, t_p, a.shape[-1])
    o_a = _sb_prompt(bias, shp(q), shp(kb), shp(vb), ntri)
    un, wn, qg, kg, att, cg = _gdn_prep(shp(conv), shp(ab), alog_pad, dtb_pad, lincl)
    o_b, ssm_p = _gdn_scan(un, wn, qg, kg, att, cg, shp(z), gw)
    y_p = _tail(xp, o_a.reshape(-1, SB_DIM), o_b.reshape(-1, GDN_DIM), sg, lw, norm_f, 1024, 512)
    y_prompt = y_p.reshape(b_p, t_p, D_MODEL).astype(x_prompt.dtype)
    as_pages = lambda a_t: jnp.transpose(
        a_t.reshape(b_p, t_p // page, SB_HEADS, SB_HEAD_DIM, page), (0, 1, 4, 2, 3))[None]
    new_k_prompt = as_pages(k_t).astype(cache_k.dtype)
    new_v_prompt = as_pages(v_t).astype(cache_v.dtype)
    seq_tails = u_tail.reshape(b_p, t_p // tm_p, SUBLANES, CONV_DIM)[:, -1]
    new_conv_prompt = seq_tails[None, :, SUBLANES - (CONV_WIDTH - 1):, :].astype(state_conv.dtype)
    new_ssm_prompt = ssm_p[None].astype(state_ssm.dtype)

    xs = x_sample.reshape(b_s * t_s, D_MODEL).astype(F32)
    tm_s = min(256, b_s * t_s)
    q, k, v, _, _, u, z, ab, sg = _layer_common(xs, lw, tm_s)
    shs = lambda a: a.reshape(b_s, t_s, a.shape[-1])
    pad8 = lambda a: jnp.pad(shs(a), ((0, 0), (0, SUBLANES - t_s), (0, 0)))
    pages_t = lambda c: jnp.transpose(c[l], (0, 2, 3, 1)).reshape(n_pool, SB_DIM, page).astype(F32)
    o_a = _sb_decode(page_table, bias, shs(q), pad8(k), pad8(v), ntri, pages_t(cache_k), pages_t(cache_v))
    o_b, ssm_s = _gdn_decode(shs(u), state_conv[l].astype(F32), shs(ab), shs(z), state_ssm[l].astype(F32),
                             cw, alog_pad, dtb_pad, gw)
    y_s = _tail(xs, o_a.reshape(-1, SB_DIM), o_b.reshape(-1, GDN_DIM), sg, lw, norm_f, tm_s, 512)
    y_sample = y_s.reshape(b_s, t_s, D_MODEL).astype(x_sample.dtype)
    head_shape = (depth, b_s, t_s, SB_HEADS, SB_HEAD_DIM)
    new_k_sample = k.reshape(head_shape).astype(cache_k.dtype)
    new_v_sample = v.reshape(head_shape).astype(cache_v.dtype)
    new_conv_sample = jnp.concatenate([state_conv[l].astype(F32), shs(u)], axis=1)[None, :, t_s:, :].astype(state_conv.dtype)
    new_ssm_sample = ssm_s[None].astype(state_ssm.dtype)

    return (y_prompt, y_sample, new_k_prompt, new_v_prompt, new_k_sample, new_v_sample,
            new_conv_prompt, new_conv_sample, new_ssm_prompt, new_ssm_sample)
```

```python
import functools
import math

import jax
import jax.numpy as jnp
from jax import lax
from jax.experimental import pallas as pl
from jax.experimental.pallas import tpu as pltpu

F32 = jnp.float32
BF16 = jnp.bfloat16

D_MODEL = 1024
SB_HEADS = 8
SB_HEAD_DIM = 64
SB_DIM = SB_HEADS * SB_HEAD_DIM
GDN_HEADS = 4
GDN_HEAD_DIM = 128
GDN_DIM = GDN_HEADS * GDN_HEAD_DIM
CONV_WIDTH = 4
CONV_DIM = 3 * GDN_DIM
GDN_CHUNK = 64
D_FF = 4 * D_MODEL
NORM_EPS = 1e-6
L2_EPS = 1e-6
NEG_LOG2E = -1.4426950408889634

LANES = 128
SUBLANES = 8
VMEM_LIMIT_BYTES = 56 * 1024 * 1024

COL_Q = 0
COL_K = COL_Q + SB_DIM
COL_V = COL_K + SB_DIM
COL_U = COL_V + SB_DIM
COL_Z = COL_U + CONV_DIM
COL_GA = COL_Z + GDN_DIM
COL_GB = COL_GA + D_MODEL
COL_AB = COL_GB + D_MODEL
IN_COLS = COL_AB + LANES

SB_BLOCK = 256
SB_QBLOCK = 512
SB_ROW_CHUNK = 64
PREP_ROWS = 512
SCAN_ROWS = 128
MERGE_ROWS = 512
DEC_GROUP = 8


def _dot(a, b):
    return jnp.dot(a, b, preferred_element_type=F32)


def _dot_nt(a, b):
    return lax.dot_general(a, b, (((1,), (1,)), ((), ())), preferred_element_type=F32)


def _dot_tn(a, b):
    return lax.dot_general(a, b, (((0,), (0,)), ((), ())), preferred_element_type=F32)


def _dot_exact01(m01, x):
    h = x.astype(BF16)
    r = x - h.astype(F32)
    m = r.astype(BF16)
    l = (r - m.astype(F32)).astype(BF16)
    return _dot(m01, h) + (_dot(m01, m) + _dot(m01, l))


def _softplus(z):
    return jnp.maximum(z, 0.0) + jnp.log(1.0 + jnp.exp2(jnp.abs(z) * NEG_LOG2E))


def _sigmoid(z):
    return 1.0 / (1.0 + jnp.exp(-z))


def _silu(z):
    return z * _sigmoid(z)


def _rmsnorm(x, w):
    return x * lax.rsqrt(jnp.mean(x * x, axis=-1, keepdims=True) + NORM_EPS) * w


def _const_spec(shape):
    nd = len(shape)
    return pl.BlockSpec(shape, lambda *_: (0,) * nd, pipeline_mode=pl.Buffered(1))


def _params(*sem):
    return pltpu.CompilerParams(dimension_semantics=sem, vmem_limit_bytes=VMEM_LIMIT_BYTES)


def _inproj_kernel(page, tiles_per_seq, x_ref, nw_ref, w_ref, *refs):
    if tiles_per_seq is None:
        q_ref, k_ref, v_ref, kb_ref, vb_ref, u_ref, z_ref, ab_ref, sg_ref = refs
    else:
        cw_ref, q_ref, k_ref, v_ref, kb_ref, vb_ref, u_ref, z_ref, ab_ref, sg_ref, tail_ref, ext_ref = refs
    x = x_ref[...]
    hb = _rmsnorm(x, nw_ref[...]).astype(BF16)

    def seg(lo, width):
        return _dot(hb, w_ref[:, lo:lo + width])

    def store_kv(ref, val):
        if page is None:
            ref[...] = val
        else:
            val_t = val.T
            for p in range(val.shape[0] // page):
                ref[p] = val_t[:, p * page:(p + 1) * page]

    fused_conv = tiles_per_seq is not None
    u_cols = [slice(j * SB_DIM, (j + 1) * SB_DIM) for j in range(CONV_DIM // SB_DIM)]
    if fused_conv:
        tm = x.shape[0]

        @pl.when(pl.program_id(0) % tiles_per_seq == 0)
        def _():
            ext_ref[0:SUBLANES, :] = jnp.zeros((SUBLANES, CONV_DIM), F32)

        for j, cols in enumerate(u_cols):
            ext_ref[SUBLANES:, cols] = seg(COL_U + j * SB_DIM, SB_DIM)
    else:
        for j, cols in enumerate(u_cols):
            u_ref[:, cols] = seg(COL_U + j * SB_DIM, SB_DIM)

    def conv_chunk(j):
        if fused_conv:
            u_ref[:, u_cols[j]] = _conv_silu(ext_ref, SUBLANES - (CONV_WIDTH - 1), tm, cw_ref, u_cols[j])

    conv_chunk(0)
    q_ref[...] = (seg(COL_Q, SB_DIM) * (SB_HEAD_DIM ** -0.5)).astype(BF16)
    conv_chunk(1)
    k = seg(COL_K, SB_DIM)
    store_kv(k_ref, k)
    kb_ref[...] = k.astype(BF16)
    conv_chunk(2)
    v = seg(COL_V, SB_DIM)
    store_kv(v_ref, v)
    vb_ref[...] = v.astype(BF16)
    if fused_conv:
        last_rows = ext_ref[tm:tm + SUBLANES, :]
        tail_ref[0] = last_rows
        ext_ref[0:SUBLANES, :] = last_rows
    z_ref[...] = seg(COL_Z, GDN_DIM).astype(BF16)
    for j in range(2 * D_MODEL // SB_DIM):
        sg_ref[:, j * SB_DIM:(j + 1) * SB_DIM] = _sigmoid(seg(COL_GA + j * SB_DIM, SB_DIM)).astype(BF16)
    ab_ref[...] = seg(COL_AB, LANES)


def _inproj(x2d, norm_w, w_perm, tm, page=None, conv_w=None, seq_len=None):
    n = x2d.shape[0]
    row = lambda width: pl.BlockSpec((tm, width), lambda i: (i, 0))
    fused_conv = conv_w is not None
    if page is None:
        kv_shape, kv_spec = jax.ShapeDtypeStruct((n, SB_DIM), F32), row(SB_DIM)
    else:
        kv_shape = jax.ShapeDtypeStruct((n // page, SB_DIM, page), F32)
        kv_spec = pl.BlockSpec((tm // page, SB_DIM, page), lambda i: (i, 0, 0))
    out_shapes = (
        jax.ShapeDtypeStruct((n, SB_DIM), BF16),
        kv_shape,
        kv_shape,
        jax.ShapeDtypeStruct((n, SB_DIM), BF16),
        jax.ShapeDtypeStruct((n, SB_DIM), BF16),
        jax.ShapeDtypeStruct((n, CONV_DIM), F32),
        jax.ShapeDtypeStruct((n, GDN_DIM), BF16),
        jax.ShapeDtypeStruct((n, LANES), F32),
        jax.ShapeDtypeStruct((n, 2 * D_MODEL), BF16),
    )
    out_specs = tuple(kv_spec if i in (1, 2) else row(s.shape[1]) for i, s in enumerate(out_shapes))
    in_specs = [row(D_MODEL), _const_spec((1, D_MODEL)), _const_spec((D_MODEL, IN_COLS))]
    operands = (x2d, norm_w, w_perm)
    scratch = []
    if fused_conv:
        in_specs.append(_const_spec((CONV_WIDTH, CONV_DIM)))
        operands += (conv_w,)
        out_shapes += (jax.ShapeDtypeStruct((n // tm, SUBLANES, CONV_DIM), F32),)
        out_specs += (pl.BlockSpec((1, SUBLANES, CONV_DIM), lambda i: (i, 0, 0)),)
        scratch = [pltpu.VMEM((tm + SUBLANES, CONV_DIM), F32)]
    return pl.pallas_call(
        functools.partial(_inproj_kernel, page, seq_len // tm if fused_conv else None),
        grid=(n // tm,),
        in_specs=in_specs,
        out_specs=out_specs,
        out_shape=out_shapes,
        scratch_shapes=scratch,
        compiler_params=_params("arbitrary" if fused_conv else "parallel"),
        name="inproj",
    )(*operands)


def _sb_tile(z, carry, ntri, mask):
    sp = _softplus(z)
    if mask is not None:
        sp = jnp.where(mask, sp, 0.0)
    suffix = _dot(sp.astype(BF16), ntri)
    w = jnp.exp(z - sp + suffix + carry)
    if mask is not None:
        w = jnp.where(mask, w, 0.0)
    return w.astype(BF16), carry - jnp.sum(sp, axis=1, keepdims=True)


def _sb_prompt_kernel(bias_ref, q_ref, k_ref, v_ref, ntri_ref, o_ref,
                      qs_scr, bias_scr, z_scr, beta_scr, decay_scr, run_scr, acc_scr):
    hp = pl.program_id(1)
    tq, tk = SB_QBLOCK, SB_BLOCK
    ratio = tq // tk
    n_q = q_ref.shape[1] // tq
    n_tiles = ratio * n_q * (n_q + 1) // 2
    lane = lax.broadcasted_iota(jnp.int32, (1, LANES), 1)
    head_lanes = [lane < SB_HEAD_DIM, lane >= SB_HEAD_DIM]
    zero = jnp.zeros((), BF16)
    rows = [slice(j * tq, (j + 1) * tq) for j in range(2)]

    for ref in (z_scr, beta_scr, decay_scr, run_scr, acc_scr):
        ref[...] = jnp.zeros_like(ref)
    row = lax.broadcasted_iota(jnp.int32, (tq, tk), 0)
    col = lax.broadcasted_iota(jnp.int32, (tq, tk), 1)
    for j in range(2):
        bias = bias_ref[2 * hp + j]
        bias_scr[0, j] = jnp.full((tq, tk), bias, F32)
        for m in range(ratio):
            bias_scr[1 + m, j] = jnp.where(m * tk + col < row, bias, -1e30)
    for i in range(n_q):
        q = q_ref[0, i * tq:(i + 1) * tq, :]
        for j in range(2):
            qs_scr[i, rows[j], :] = jnp.where(head_lanes[j], q, zero)

    def key_block(ref, kb):
        return ref[0, pl.ds(pl.multiple_of(kb * tk, tk), tk), :]

    def step(s, tiles):
        (q_a, k_a), (q_b, k_b), (q_d, k_d) = tiles
        chunks = [slice(c * SB_ROW_CHUNK, (c + 1) * SB_ROW_CHUNK) for c in range(2 * tq // SB_ROW_CHUNK)]
        zs = [z_scr[r, :] for r in chunks]
        z_scr[...] = _dot_nt(qs_scr[q_a], key_block(k_ref, k_a))
        valid_d = jnp.logical_and(s >= 2, s - 2 < n_tiles)
        v = key_block(v_ref, k_d)
        w = jnp.concatenate([beta_scr[rows[j], :] * decay_scr[rows[j], :] for j in range(2)], axis=1)
        vm = jnp.concatenate(
            [jnp.where(jnp.logical_and(head_lanes[j], valid_d), v, zero) for j in range(2)], axis=0)
        pv = _dot(w, vm)
        acc_rows = pl.ds(pl.multiple_of(q_d * tq, tq), tq)
        acc_scr[acc_rows, :] = acc_scr[acc_rows, :] + pv
        first_of_block = k_b == ratio * q_b + ratio - 1
        offset = jnp.maximum(k_b - ratio * q_b + 1, 0)
        one = jnp.ones((), BF16)
        sps = []
        for r, z_raw in zip(chunks, zs):
            j, r_in_head = divmod(r.start, tq)
            z = z_raw + bias_scr[offset, j, r_in_head:r_in_head + SB_ROW_CHUNK, :]
            t = jnp.exp2(jnp.abs(z) * NEG_LOG2E)
            log1p_t = jnp.log(1.0 + t).astype(BF16)
            z16, t16 = z.astype(BF16), t.astype(BF16)
            sps.append(jnp.maximum(z16, zero) + log1p_t)
            beta_scr[r, :] = jnp.where(z16 >= zero, one, t16) / (one + t16)
        suffix = _dot(jnp.concatenate(sps, axis=0), ntri_ref[...])
        for r, sp in zip(chunks, sps):
            run = jnp.where(first_of_block, 0.0, run_scr[r, :])
            decay_scr[r, :] = jnp.exp(suffix[r, :] + run).astype(BF16)
            run_scr[r, :] = run + suffix[r, 0:1] - sp[:, 0:1].astype(F32)
        wrap = k_a == 0
        q_n = jnp.where(wrap, jnp.minimum(q_a + 1, n_q - 1), q_a)
        k_n = jnp.where(wrap, ratio * q_n + ratio - 1, k_a - 1)
        return ((q_n, k_n),) + tiles[:2]

    first = (jnp.int32(0), jnp.int32(ratio - 1))
    lax.fori_loop(0, n_tiles + 2, step, (first,) * 3)
    o_ref[0] = acc_scr[...].astype(BF16)


def _sb_prompt(sb_bias, q, k, v, ntri):
    bsz, t_len, _ = q.shape
    tq, tk = SB_QBLOCK, SB_BLOCK
    seq = pl.BlockSpec((1, t_len, LANES), lambda b, h, *_: (b, 0, h))
    return pl.pallas_call(
        _sb_prompt_kernel,
        grid_spec=pltpu.PrefetchScalarGridSpec(
            num_scalar_prefetch=1,
            grid=(bsz, SB_DIM // LANES),
            in_specs=[seq, seq, seq, pl.BlockSpec((tk, tk), lambda b, h, *_: (0, 0))],
            out_specs=seq,
            scratch_shapes=[
                pltpu.VMEM((t_len // tq, 2 * tq, LANES), BF16),
                pltpu.VMEM((1 + tq // tk, 2, tq, tk), F32),
                pltpu.VMEM((2 * tq, tk), F32),
                pltpu.VMEM((2 * tq, tk), BF16),
                pltpu.VMEM((2 * tq, tk), BF16),
                pltpu.VMEM((2 * tq, 1), F32),
                pltpu.VMEM((t_len, LANES), F32),
            ],
        ),
        out_shape=jax.ShapeDtypeStruct((bsz, t_len, SB_DIM), BF16),
        compiler_params=_params("parallel", "parallel"),
        name="sb_prompt",
    )(sb_bias, q, k, v, ntri)


def _sb_decode_kernel(n_pages, t_new, pt_ref, bias_ref, q_ref, kn_ref, vn_ref, ntri_ref, *refs):
    kt_pages = refs[:n_pages]
    vt_pages = refs[n_pages:2 * n_pages]
    o_ref = refs[2 * n_pages]
    page = kt_pages[0].shape[2]
    rows = t_new * SB_HEADS
    r_iota = lax.broadcasted_iota(jnp.int32, (rows, SB_DIM), 0)
    l_iota = lax.broadcasted_iota(jnp.int32, (rows, SB_DIM), 1)
    head_of_row = r_iota % SB_HEADS
    own_lanes = (l_iota // SB_HEAD_DIM) == head_of_row

    q = q_ref[0].astype(F32)
    q_rep = jnp.broadcast_to(q[:, None, :], (t_new, SB_HEADS, SB_DIM)).reshape(rows, SB_DIM)
    q_bd = jnp.where(own_lanes, q_rep, 0.0).astype(BF16)

    r1 = lax.broadcasted_iota(jnp.int32, (rows, 1), 0)
    bias = jnp.zeros((rows, 1), F32)
    for h in range(SB_HEADS):
        bias = jnp.where(r1 % SB_HEADS == h, bias_ref[h], bias)

    pad = jnp.zeros((page - SUBLANES, SB_DIM), F32)
    k_new = jnp.concatenate([kn_ref[0], pad], axis=0).astype(BF16)
    v_new = jnp.concatenate([vn_ref[0], pad], axis=0).astype(BF16)
    kcol = lax.broadcasted_iota(jnp.int32, (rows, page), 1)
    qtok = lax.broadcasted_iota(jnp.int32, (rows, page), 0) // SB_HEADS
    z = _dot_nt(q_bd, k_new) + bias
    w, carry = _sb_tile(z, jnp.zeros((rows, 1), F32), ntri_ref[:page, :page], kcol < qtok)
    acc = _dot(w, v_new)

    blk = ntri_ref.shape[0] // page
    order = range(n_pages // blk - 1, -1, -1)
    lane_cat = lambda pages, pb: jnp.concatenate(
        [pages[pb * blk + i][0] for i in range(blk)], axis=1).astype(BF16)
    zs = [_dot(q_bd, lane_cat(kt_pages, pb)) + bias for pb in order]
    sps = [_softplus(z) for z in zs]
    suffixes = [_dot(sp.astype(BF16), ntri_ref[...]) for sp in sps]
    for pb, z, sp, suffix in zip(order, zs, sps, suffixes):
        w = jnp.exp(z - sp + suffix + carry).astype(BF16)
        carry = carry - jnp.sum(sp, axis=1, keepdims=True)
        acc = acc + _dot_nt(w, lane_cat(vt_pages, pb))

    acc = jnp.where(own_lanes, acc, 0.0)
    o_ref[0] = jnp.sum(acc.reshape(t_new, SB_HEADS, SB_DIM), axis=1).astype(BF16)


def _sb_decode(page_table, sb_bias, q, k_new8, v_new8, ntri, cache_kt, cache_vt):
    n_seq, t_new, _ = q.shape
    n_pages = page_table.shape[1]
    page = cache_kt.shape[2]

    def page_spec(p):
        return pl.BlockSpec((1, SB_DIM, page), lambda s, pt, b: (pt[s, p], 0, 0))

    seq_spec = lambda r: pl.BlockSpec((1, r, SB_DIM), lambda s, pt, b: (s, 0, 0))
    return pl.pallas_call(
        functools.partial(_sb_decode_kernel, n_pages, t_new),
        grid_spec=pltpu.PrefetchScalarGridSpec(
            num_scalar_prefetch=2,
            grid=(n_seq,),
            in_specs=[seq_spec(t_new), seq_spec(SUBLANES), seq_spec(SUBLANES),
                      pl.BlockSpec(ntri.shape, lambda s, pt, b: (0, 0))]
                     + [page_spec(p) for p in range(n_pages)] * 2,
            out_specs=seq_spec(t_new),
        ),
        out_shape=jax.ShapeDtypeStruct((n_seq, t_new, SB_DIM), BF16),
        compiler_params=_params("parallel"),
        name="sb_decode",
    )(page_table, sb_bias, q, k_new8, v_new8, ntri, *([cache_kt] * n_pages), *([cache_vt] * n_pages))


def _conv_silu(ext_ref, first, rows, cw_ref, cols=slice(None)):
    acc = None
    for w in range(CONV_WIDTH):
        term = ext_ref[first + w:first + w + rows, cols] * cw_ref[w:w + 1, cols]
        acc = term if acc is None else acc + term
    return _silu(acc)


def _l2norm(x):
    return x * lax.rsqrt(jnp.sum(x * x, axis=-1, keepdims=True) + L2_EPS)


def _gate_terms(ab, alog_ref, dtb_ref):
    g = -jnp.exp(alog_ref[...]) * _softplus(ab + dtb_ref[...])
    return g, _sigmoid(ab)


def _gated_out_norm(o, z, gw):
    o = o * lax.rsqrt(jnp.mean(o * o, axis=-1, keepdims=True) + NORM_EPS) * gw
    return o * _silu(z)


def _gdn_prep_kernel(conv_ref, ab_ref, alog_ref, dtb_ref, lincl_ref, hones_ref,
                     un_ref, wn_ref, qg_ref, kg_ref, att_ref, cg_ref):
    rows = conv_ref.shape[1]
    c_len = GDN_CHUNK
    dk = GDN_HEAD_DIM
    conv = conv_ref[0]

    g_all, beta_all = _gate_terms(ab_ref[0], alog_ref, dtb_ref)
    r_i = lax.broadcasted_iota(jnp.int32, (c_len, c_len), 0)
    c_i = lax.broadcasted_iota(jnp.int32, (c_len, c_len), 1)
    incl = c_i <= r_i
    strict = c_i < r_i
    lincl = lincl_ref[...]
    n_chunks = rows // c_len

    cg_chunks = [_dot_exact01(lincl, g_all[c * c_len:(c + 1) * c_len, :]) for c in range(n_chunks)]
    cg_t_pairs = []
    for pair in range(n_chunks // 2):
        cg_pair = jnp.concatenate(cg_chunks[2 * pair:2 * pair + 2], axis=0)
        cg_ref[0, 2 * pair * c_len:(2 * pair + 2) * c_len, :] = cg_pair
        cg_t_pairs.append(cg_pair.T)

    problems = [(c, h) for c in range(n_chunks) for h in range(GDN_HEADS)]

    def head_cols(base, h):
        return slice(base + h * dk, base + (h + 1) * dk)

    def l2norm_heads(x):
        sq = x * x
        hi = sq.astype(BF16)
        lo = (sq - hi.astype(F32)).astype(BF16)
        return x * lax.rsqrt(_dot(hi, hones_ref[...]) + _dot(lo, hones_ref[...]) + L2_EPS)

    q_all = l2norm_heads(conv[:, :GDN_DIM]) * (dk ** -0.5)
    k_all = l2norm_heads(conv[:, GDN_DIM:2 * GDN_DIM])

    qs, ks, vs, bcs, cgcols, decays = [], [], [], [], [], []
    for c, h in problems:
        r = slice(c * c_len, (c + 1) * c_len)
        qs.append(q_all[r, head_cols(0, h)])
        ks.append(k_all[r, head_cols(0, h)])
        vs.append(conv[r, head_cols(2 * GDN_DIM, h)])
        bcs.append(beta_all[r, GDN_HEADS + h:GDN_HEADS + h + 1])
        cg_col = cg_chunks[c][:, h:h + 1]
        cg_row = cg_t_pairs[c // 2][h:h + 1, (c % 2) * c_len:(c % 2 + 1) * c_len]
        cgcols.append(cg_col)
        decays.append(jnp.exp(jnp.where(incl, cg_col - cg_row, -jnp.inf)))
    kbs = [k.astype(BF16) for k in ks]
    kks = [_dot_nt(kb, kb) for kb in kbs]

    powers = [jnp.where(strict, -(bc * kk * dec), 0.0) for bc, kk, dec in zip(bcs, kks, decays)]
    inv_off = powers
    for _ in range(int(math.log2(c_len)) - 1):
        pbs = [p.astype(BF16) for p in powers]
        powers = [_dot(pb, pb) for pb in pbs]
        next_pbs = [p.astype(BF16) for p in powers]
        inv_off = [t + p + _dot(t.astype(BF16), pb) for t, p, pb in zip(inv_off, powers, next_pbs)]

    for (c, h), q, k, v, bc, cg_col, dec, t_off in zip(problems, qs, ks, vs, bcs, cgcols, decays, inv_off):
        r = slice(c * c_len, (c + 1) * c_len)
        cols = head_cols(0, h)
        rhs = jnp.concatenate([v * bc, k * (bc * jnp.exp(cg_col))], axis=1)
        sol = rhs + _dot(t_off.astype(BF16), rhs.astype(BF16))
        un_ref[0, r, cols] = sol[:, :dk]
        wn_ref[0, r, cols] = sol[:, dk:].astype(BF16)
        qg_ref[0, r, cols] = (q * jnp.exp(cg_col)).astype(BF16)
        g_last = cg_chunks[c][c_len - 1:c_len, h:h + 1]
        kg_ref[0, r, cols] = (k * jnp.exp(g_last - cg_col)).astype(BF16)
        qk = _dot_nt(q.astype(BF16), k.astype(BF16))
        att_ref[0, h, r, :] = (qk * dec).astype(BF16)


def _gdn_prep(conv, ab, alog_pad, dtb_pad, lincl):
    bsz, t_len, _ = conv.shape
    head_of = jnp.arange(GDN_DIM) // GDN_HEAD_DIM
    head_ones = (head_of[:, None] == head_of[None, :]).astype(BF16)
    rows = min(PREP_ROWS, t_len)
    blk = lambda width: pl.BlockSpec((1, rows, width), lambda b, i: (b, i, 0))
    out_shapes = (
        jax.ShapeDtypeStruct((bsz, t_len, GDN_DIM), F32),
        jax.ShapeDtypeStruct((bsz, t_len, GDN_DIM), BF16),
        jax.ShapeDtypeStruct((bsz, t_len, GDN_DIM), BF16),
        jax.ShapeDtypeStruct((bsz, t_len, GDN_DIM), BF16),
        jax.ShapeDtypeStruct((bsz, GDN_HEADS, t_len, GDN_CHUNK), BF16),
        jax.ShapeDtypeStruct((bsz, t_len, LANES), F32),
    )
    return pl.pallas_call(
        _gdn_prep_kernel,
        grid=(bsz, t_len // rows),
        in_specs=[
            blk(CONV_DIM),
            blk(LANES),
            _const_spec((1, LANES)),
            _const_spec((1, LANES)),
            _const_spec((GDN_CHUNK, GDN_CHUNK)),
            _const_spec((GDN_DIM, GDN_DIM)),
        ],
        out_specs=(blk(GDN_DIM), blk(GDN_DIM), blk(GDN_DIM), blk(GDN_DIM),
                   pl.BlockSpec((1, GDN_HEADS, rows, GDN_CHUNK), lambda b, i: (b, 0, i, 0)),
                   blk(LANES)),
        out_shape=out_shapes,
        compiler_params=_params("parallel", "parallel"),
        name="gdn_prep",
    )(conv, ab, alog_pad, dtb_pad, lincl, head_ones)


def _gdn_scan_kernel(un_ref, wn_ref, qg_ref, kg_ref, att_ref, cg_ref, z_ref, gw_ref, o_ref, s_out_ref, s_ref):
    c = pl.program_id(0)
    bsz = un_ref.shape[0]
    c_len = GDN_CHUNK
    dk = GDN_HEAD_DIM

    @pl.when(c == 0)
    def _():
        s_ref[...] = jnp.zeros_like(s_ref)

    chains = [(b, h, slice(h * dk, (h + 1) * dk)) for b in range(bsz) for h in range(GDN_HEADS)]
    states = [s_ref[b, h] for b, h, _ in chains]
    for sub in range(un_ref.shape[1] // c_len):
        r = slice(sub * c_len, (sub + 1) * c_len)
        decay_last = [jnp.exp(cg_ref[b, r.stop - 1:r.stop, :]) for b in range(bsz)]
        sbs = [s.astype(BF16) for s in states]
        ws = [_dot(wn_ref[b, r, cols], sb) for (b, _, cols), sb in zip(chains, sbs)]
        qs = [_dot(qg_ref[b, r, cols], sb) for (b, _, cols), sb in zip(chains, sbs)]
        vbs = [(un_ref[b, r, cols] - w).astype(BF16) for (b, _, cols), w in zip(chains, ws)]
        new_states = []
        for (b, h, cols), s, q_s, vb in zip(chains, states, qs, vbs):
            o = q_s + _dot(att_ref[b, h, r, :], vb)
            new_states.append(s * decay_last[b][:, h:h + 1] + _dot_tn(kg_ref[b, r, cols], vb))
            o_ref[b, r, cols] = _gated_out_norm(o, z_ref[b, r, cols].astype(F32), gw_ref[...]).astype(BF16)
        states = new_states
    for (b, h, _), s in zip(chains, states):
        s_ref[b, h] = s

    @pl.when(c == pl.num_programs(0) - 1)
    def _():
        s_out_ref[...] = s_ref[...]


def _gdn_scan(un, wn, qg, kg, att, cg, z, gw):
    bsz, t_len, _ = un.shape
    rows = min(SCAN_ROWS, t_len)
    blk = lambda width: pl.BlockSpec((bsz, rows, width), lambda c: (0, c, 0))
    state_shape = (bsz, GDN_HEADS, GDN_HEAD_DIM, GDN_HEAD_DIM)
    return pl.pallas_call(
        _gdn_scan_kernel,
        grid=(t_len // rows,),
        in_specs=[blk(GDN_DIM), blk(GDN_DIM), blk(GDN_DIM), blk(GDN_DIM),
                  pl.BlockSpec((bsz, GDN_HEADS, rows, GDN_CHUNK), lambda c: (0, 0, c, 0)),
                  blk(LANES), blk(GDN_DIM), _const_spec((1, GDN_HEAD_DIM))],
        out_specs=(blk(GDN_DIM), pl.BlockSpec(state_shape, lambda c: (0, 0, 0, 0))),
        out_shape=(jax.ShapeDtypeStruct((bsz, t_len, GDN_DIM), BF16),
                   jax.ShapeDtypeStruct(state_shape, F32)),
        scratch_shapes=[pltpu.VMEM(state_shape, F32)],
        compiler_params=_params("arbitrary"),
        name="gdn_scan",
    )(un, wn, qg, kg, att, cg, z, gw)


def _gdn_decode_kernel(t_new, u_ref, hist_ref, ab_ref, z_ref, s_in_ref, cw_ref, alog_ref, dtb_ref, gw_ref,
                       o_ref, s_out_ref, ext_ref, ab_scr):
    dk = GDN_HEAD_DIM
    n_hist = CONV_WIDTH - 1
    n_seq = u_ref.shape[0]
    tile = (SUBLANES, LANES)
    tail = jnp.zeros((LANES - SUBLANES, LANES), F32)
    pad_rows = lambda x: jnp.concatenate([x, tail], axis=0)
    row = lax.broadcasted_iota(jnp.int32, tile, 0)
    lane = lax.broadcasted_iota(jnp.int32, tile, 1)
    real = row < t_new
    ext_ref[...] = jnp.zeros_like(ext_ref)
    ab_scr[...] = jnp.zeros_like(ab_scr)

    problems = [(g, h) for g in range(n_seq) for h in range(GDN_HEADS)]
    q8, k8, v8, e_col, b_col, w_col, decay, e_last = {}, {}, {}, {}, {}, {}, {}, {}
    for g in range(n_seq):
        ext_ref[g, 0:n_hist, :] = hist_ref[g]
        ext_ref[g, n_hist:n_hist + t_new, :] = u_ref[g]
        ab_scr[g, 0:t_new, :] = ab_ref[g]
        conv = _conv_silu(ext_ref.at[g], 0, SUBLANES, cw_ref)
        g_all, beta_all = _gate_terms(ab_scr[g], alog_ref, dtb_ref)
        cg = g_all
        for shift in range(1, t_new):
            cg = cg + jnp.where(row >= shift, pltpu.roll(g_all, shift, axis=0), 0.0)
        cg_last = cg[t_new - 1:t_new, :]
        e_all = jnp.exp(cg)
        w_all = jnp.exp(cg_last - cg)
        cg_t = pad_rows(cg).T
        for h in range(GDN_HEADS):
            p = (g, h)
            q8[p] = _l2norm(conv[:, h * dk:(h + 1) * dk]) * (dk ** -0.5)
            k8[p] = _l2norm(conv[:, GDN_DIM + h * dk:GDN_DIM + (h + 1) * dk])
            v8[p] = conv[:, 2 * GDN_DIM + h * dk:2 * GDN_DIM + (h + 1) * dk]
            e_col[p] = e_all[:, h:h + 1]
            b_col[p] = beta_all[:, GDN_HEADS + h:GDN_HEADS + h + 1]
            w_col[p] = w_all[:, h:h + 1]
            e_last[p] = e_all[t_new - 1:t_new, h:h + 1]
            decay[p] = jnp.exp(jnp.where(lane <= row, cg[:, h:h + 1] - cg_t[h:h + 1, :], -jnp.inf))

    states = {p: s_in_ref[p[0], p[1]] for p in problems}
    kq = {p: jnp.concatenate([k8[p], q8[p]], axis=0).astype(BF16) for p in problems}
    ks0_qs0 = {p: _dot(kq[p], states[p].astype(BF16)) for p in problems}
    gram = {p: _dot_nt(kq[p], pad_rows(k8[p]).astype(BF16)) for p in problems}
    deltas = {}
    for p in problems:
        m = b_col[p] * jnp.where(lane < row, gram[p][:SUBLANES] * decay[p], 0.0)
        r = b_col[p] * (v8[p] - e_col[p] * ks0_qs0[p][:SUBLANES])
        for j in range(t_new - 1):
            r = r - m[:, j:j + 1] * r[j:j + 1, :]
        deltas[p] = jnp.where(real, r, 0.0)
    updates = {}
    for p in problems:
        kw_t = pad_rows(jnp.where(real, k8[p] * w_col[p], 0.0)).T
        updates[p] = _dot(kw_t.astype(BF16), pad_rows(deltas[p]).astype(BF16))
    for g, h in problems:
        p = (g, h)
        cols = slice(h * dk, (h + 1) * dk)
        coef = gram[p][SUBLANES:] * decay[p]
        o = e_col[p] * ks0_qs0[p][SUBLANES:]
        for j in range(t_new):
            o = o + coef[:, j:j + 1] * deltas[p][j:j + 1, :]
        o = _gated_out_norm(o[:t_new], z_ref[g, :, cols].astype(F32), gw_ref[...])
        o_ref[g, :, cols] = o.astype(BF16)
        s_out_ref[g, h] = states[p] * e_last[p] + updates[p]


def _gdn_decode(u, hist, ab, z, s_in, conv_w, alog_pad, dtb_pad, gw):
    n_seq, t_new, _ = u.shape
    grp = DEC_GROUP
    seq = lambda r, width: pl.BlockSpec((grp, r, width), lambda i: (i, 0, 0))
    state = pl.BlockSpec((grp, GDN_HEADS, GDN_HEAD_DIM, GDN_HEAD_DIM), lambda i: (i, 0, 0, 0))
    return pl.pallas_call(
        functools.partial(_gdn_decode_kernel, t_new),
        grid=(n_seq // grp,),
        in_specs=[seq(t_new, CONV_DIM), seq(CONV_WIDTH - 1, CONV_DIM), seq(t_new, LANES), seq(t_new, GDN_DIM),
                  state, _const_spec((CONV_WIDTH, CONV_DIM)), _const_spec((1, LANES)), _const_spec((1, LANES)),
                  _const_spec((1, GDN_HEAD_DIM))],
        out_specs=(seq(t_new, GDN_DIM), state),
        out_shape=(jax.ShapeDtypeStruct((n_seq, t_new, GDN_DIM), BF16),
                   jax.ShapeDtypeStruct(s_in.shape, F32)),
        scratch_shapes=[pltpu.VMEM((grp, 2 * SUBLANES, CONV_DIM), F32), pltpu.VMEM((grp, SUBLANES, LANES), F32)],
        compiler_params=_params("parallel"),
        name="gdn_decode",
    )(u, hist, ab, z, s_in, conv_w, alog_pad, dtb_pad, gw)


def _tail_kernel(x_ref, oa_ref, ob_ref, sg_ref, wpa_ref, wpb_ref, wo_ref, nw_ref, wup_ref, wdown_ref, nf_ref,
                 y_ref, acc_ref, hm_ref):
    f = pl.program_id(1)

    @pl.when(f == 0)
    def _():
        for r0 in range(0, x_ref.shape[0], MERGE_ROWS):
            r = slice(r0, min(r0 + MERGE_ROWS, x_ref.shape[0]))
            y_a = _dot(oa_ref[r, :], wpa_ref[...])
            y_b = _dot(ob_ref[r, :], wpb_ref[...])
            mix = sg_ref[r, :D_MODEL] * y_a + sg_ref[r, D_MODEL:] * y_b
            x1 = x_ref[r, :] + _dot(mix.astype(BF16), wo_ref[...])
            acc_ref[r, :] = x1
            hm_ref[r, :] = _rmsnorm(x1, nw_ref[...]).astype(BF16)

    up = jnp.maximum(_dot(hm_ref[...], wup_ref[...]), 0.0)
    acc_ref[...] += _dot((up * up).astype(BF16), wdown_ref[...])

    @pl.when(f == pl.num_programs(1) - 1)
    def _():
        y_ref[...] = _rmsnorm(acc_ref[...], nf_ref[...])


def _tail(x2d, o_a, o_b, sg, lw, norm_f, tm, tf):
    n = x2d.shape[0]
    row = lambda width: pl.BlockSpec((tm, width), lambda i, f: (i, 0))
    return pl.pallas_call(
        _tail_kernel,
        grid=(n // tm, D_FF // tf),
        in_specs=[row(D_MODEL), row(SB_DIM), row(GDN_DIM), row(2 * D_MODEL),
                  _const_spec(lw["w_pa"].shape), _const_spec(lw["w_pb"].shape), _const_spec(lw["w_o"].shape),
                  _const_spec((1, D_MODEL)),
                  pl.BlockSpec((D_MODEL, tf), lambda i, f: (0, f)),
                  pl.BlockSpec((tf, D_MODEL), lambda i, f: (f, 0)),
                  _const_spec((1, D_MODEL))],
        out_specs=row(D_MODEL),
        out_shape=jax.ShapeDtypeStruct((n, D_MODEL), F32),
        scratch_shapes=[pltpu.VMEM((tm, D_MODEL), F32), pltpu.VMEM((tm, D_MODEL), BF16)],
        compiler_params=_params("parallel", "arbitrary"),
        name="tail",
    )(x2d, o_a, o_b, sg, lw["w_pa"], lw["w_pb"], lw["w_o"], lw["norm_mlp_w"], lw["w_up"], lw["w_down"], norm_f)


def _permute_w_in(w):
    n_main = 3 * SB_DIM + CONV_DIM + GDN_DIM
    gates = w[:, n_main + 2 * GDN_HEADS:]
    ab = w[:, n_main:n_main + 2 * GDN_HEADS]
    pad = jnp.zeros((w.shape[0], LANES - 2 * GDN_HEADS), w.dtype)
    return jnp.concatenate([w[:, :n_main], gates, ab, pad], axis=1).astype(BF16)


def _lane_pad(vec, offset):
    return jnp.zeros((1, LANES), F32).at[0, offset:offset + vec.shape[0]].set(vec.astype(F32))


def _layer_common(x2d, lw, tm, **kwargs):
    return _inproj(x2d, lw["norm_mix_w"], lw["w_in"], tm, **kwargs)


def kernel(x_prompt, x_sample, cache_k, cache_v, page_table, state_conv, state_ssm, norm_mix_w, w_in, sb_bias,
           conv_w, a_log, dt_bias, gdn_norm_w, w_pa, w_pb, w_o, norm_mlp_w, w_up, w_down, norm_final_w):
    depth = w_in.shape[0]
    assert depth == 1, "the residual stream is normalised once, after the only layer"
    b_p, t_p, _ = x_prompt.shape
    b_s, t_s, _ = x_sample.shape
    n_pool, page = cache_k.shape[1], cache_k.shape[2]
    l = 0
    lw = {
        "norm_mix_w": norm_mix_w[l].reshape(1, D_MODEL).astype(F32),
        "w_in": _permute_w_in(w_in[l]),
        "w_pa": w_pa[l].astype(BF16), "w_pb": w_pb[l].astype(BF16), "w_o": w_o[l].astype(BF16),
        "norm_mlp_w": norm_mlp_w[l].reshape(1, D_MODEL).astype(F32),
        "w_up": w_up[l].astype(BF16), "w_down": w_down[l].astype(BF16),
    }
    norm_f = norm_final_w.reshape(1, D_MODEL).astype(F32)
    bias = sb_bias[l].astype(F32)
    cw = conv_w[l].astype(F32)
    alog_pad = _lane_pad(a_log[l], 0)
    dtb_pad = _lane_pad(dt_bias[l], 0)
    gw = gdn_norm_w[l].reshape(1, GDN_HEAD_DIM).astype(F32)
    ntri = -jnp.tril(jnp.ones((SB_BLOCK, SB_BLOCK), BF16), -1)
    lincl = jnp.tril(jnp.ones((GDN_CHUNK, GDN_CHUNK), BF16))

    xp = x_prompt.reshape(b_p * t_p, D_MODEL).astype(F32)
    tm_p = 256
    q, k_t, v_t, kb, vb, conv, z, ab, sg, u_tail = _layer_common(xp, lw, tm_p, page=page, conv_w=cw, seq_len=t_p)
    shp = lambda a: a.reshape(b_p, t_p, a.shape[-1])
    o_a = _sb_prompt(bias, shp(q), shp(kb), shp(vb), ntri)
    un, wn, qg, kg, att, cg = _gdn_prep(shp(conv), shp(ab), alog_pad, dtb_pad, lincl)
    o_b, ssm_p = _gdn_scan(un, wn, qg, kg, att, cg, shp(z), gw)
    y_p = _tail(xp, o_a.reshape(-1, SB_DIM), o_b.reshape(-1, GDN_DIM), sg, lw, norm_f, 1024, 512)
    y_prompt = y_p.reshape(b_p, t_p, D_MODEL).astype(x_prompt.dtype)
    as_pages = lambda a_t: jnp.transpose(
        a_t.reshape(b_p, t_p // page, SB_HEADS, SB_HEAD_DIM, page), (0, 1, 4, 2, 3))[None]
    new_k_prompt = as_pages(k_t).astype(cache_k.dtype)
    new_v_prompt = as_pages(v_t).astype(cache_v.dtype)
    seq_tails = u_tail.reshape(b_p, t_p // tm_p, SUBLANES, CONV_DIM)[:, -1]
    new_conv_prompt = seq_tails[None, :, SUBLANES - (CONV_WIDTH - 1):, :].astype(state_conv.dtype)
    new_ssm_prompt = ssm_p[None].astype(state_ssm.dtype)

    xs = x_sample.reshape(b_s * t_s, D_MODEL).astype(F32)
    tm_s = min(256, b_s * t_s)
    q, k, v, _, _, u, z, ab, sg = _layer_common(xs, lw, tm_s)
    shs = lambda a: a.reshape(b_s, t_s, a.shape[-1])
    pad8 = lambda a: jnp.pad(shs(a), ((0, 0), (0, SUBLANES - t_s), (0, 0)))
    pages_t = lambda c: jnp.transpose(c[l], (0, 2, 3, 1)).reshape(n_pool, SB_DIM, page).astype(F32)
    o_a = _sb_decode(page_table, bias, shs(q), pad8(k), pad8(v), ntri, pages_t(cache_k), pages_t(cache_v))
    o_b, ssm_s = _gdn_decode(shs(u), state_conv[l].astype(F32), shs(ab), shs(z), state_ssm[l].astype(F32),
                             cw, alog_pad, dtb_pad, gw)
    y_s = _tail(xs, o_a.reshape(-1, SB_DIM), o_b.reshape(-1, GDN_DIM), sg, lw, norm_f, tm_s, 512)
    y_sample = y_s.reshape(b_s, t_s, D_MODEL).astype(x_sample.dtype)
    head_shape = (depth, b_s, t_s, SB_HEADS, SB_HEAD_DIM)
    new_k_sample = k.reshape(head_shape).astype(cache_k.dtype)
    new_v_sample = v.reshape(head_shape).astype(cache_v.dtype)
    new_conv_sample = jnp.concatenate([state_conv[l].astype(F32), shs(u)], axis=1)[None, :, t_s:, :].astype(state_conv.dtype)
    new_ssm_sample = ssm_s[None].astype(state_ssm.dtype)

    return (y_prompt, y_sample, new_k_prompt, new_v_prompt, new_k_sample, new_v_sample,
            new_conv_prompt, new_conv_sample, new_ssm_prompt, new_ssm_sample)
```

```python
import functools
import math

import jax
import jax.numpy as jnp
from jax import lax
from jax.experimental import pallas as pl
from jax.experimental.pallas import tpu as pltpu

F32 = jnp.float32
BF16 = jnp.bfloat16

D_MODEL = 1024
SB_HEADS = 8
SB_HEAD_DIM = 64
SB_DIM = SB_HEADS * SB_HEAD_DIM
GDN_HEADS = 4
GDN_HEAD_DIM = 128
GDN_DIM = GDN_HEADS * GDN_HEAD_DIM
CONV_WIDTH = 4
CONV_DIM = 3 * GDN_DIM
GDN_CHUNK = 64
D_FF = 4 * D_MODEL
NORM_EPS = 1e-6
L2_EPS = 1e-6
NEG_LOG2E = -1.4426950408889634

LANES = 128
SUBLANES = 8
VMEM_LIMIT_BYTES = 56 * 1024 * 1024

COL_Q = 0
COL_K = COL_Q + SB_DIM
COL_V = COL_K + SB_DIM
COL_U = COL_V + SB_DIM
COL_Z = COL_U + CONV_DIM
COL_GA = COL_Z + GDN_DIM
COL_GB = COL_GA + D_MODEL
COL_AB = COL_GB + D_MODEL
IN_COLS = COL_AB + LANES

SB_BLOCK = 256
SB_QBLOCK = 512
SB_ROW_CHUNK = 64
PREP_ROWS = 512
SCAN_ROWS = 256
MERGE_ROWS = 512
DEC_GROUP = 8


def _dot(a, b):
    return jnp.dot(a, b, preferred_element_type=F32)


def _dot_nt(a, b):
    return lax.dot_general(a, b, (((1,), (1,)), ((), ())), preferred_element_type=F32)


def _dot_tn(a, b):
    return lax.dot_general(a, b, (((0,), (0,)), ((), ())), preferred_element_type=F32)


def _dot_exact01(m01, x):
    h = x.astype(BF16)
    r = x - h.astype(F32)
    m = r.astype(BF16)
    l = (r - m.astype(F32)).astype(BF16)
    return _dot(m01, h) + (_dot(m01, m) + _dot(m01, l))


def _softplus(z):
    return jnp.maximum(z, 0.0) + jnp.log(1.0 + jnp.exp2(jnp.abs(z) * NEG_LOG2E))


def _sigmoid(z):
    return 1.0 / (1.0 + jnp.exp(-z))


def _silu(z):
    return z * _sigmoid(z)


def _rmsnorm(x, w):
    return x * lax.rsqrt(jnp.mean(x * x, axis=-1, keepdims=True) + NORM_EPS) * w


def _const_spec(shape):
    nd = len(shape)
    return pl.BlockSpec(shape, lambda *_: (0,) * nd, pipeline_mode=pl.Buffered(1))


def _params(*sem):
    return pltpu.CompilerParams(dimension_semantics=sem, vmem_limit_bytes=VMEM_LIMIT_BYTES)


def _inproj_kernel(page, tiles_per_seq, x_ref, nw_ref, w_ref, *refs):
    if tiles_per_seq is None:
        q_ref, k_ref, v_ref, kb_ref, vb_ref, u_ref, z_ref, ab_ref, sg_ref = refs
    else:
        cw_ref, q_ref, k_ref, v_ref, kb_ref, vb_ref, u_ref, z_ref, ab_ref, sg_ref, tail_ref, ext_ref = refs
    x = x_ref[...]
    hb = _rmsnorm(x, nw_ref[...]).astype(BF16)

    def seg(lo, width):
        return _dot(hb, w_ref[:, lo:lo + width])

    def store_kv(ref, val):
        if page is None:
            ref[...] = val
        else:
            val_t = val.T
            for p in range(val.shape[0] // page):
                ref[p] = val_t[:, p * page:(p + 1) * page]

    fused_conv = tiles_per_seq is not None
    u_cols = [slice(j * SB_DIM, (j + 1) * SB_DIM) for j in range(CONV_DIM // SB_DIM)]
    if fused_conv:
        tm = x.shape[0]

        @pl.when(pl.program_id(0) % tiles_per_seq == 0)
        def _():
            ext_ref[0:SUBLANES, :] = jnp.zeros((SUBLANES, CONV_DIM), F32)

        for j, cols in enumerate(u_cols):
            ext_ref[SUBLANES:, cols] = seg(COL_U + j * SB_DIM, SB_DIM)
    else:
        for j, cols in enumerate(u_cols):
            u_ref[:, cols] = seg(COL_U + j * SB_DIM, SB_DIM)

    def conv_chunk(j):
        if fused_conv:
            u_ref[:, u_cols[j]] = _conv_silu(ext_ref, SUBLANES - (CONV_WIDTH - 1), tm, cw_ref, u_cols[j])

    conv_chunk(0)
    q_ref[...] = (seg(COL_Q, SB_DIM) * (SB_HEAD_DIM ** -0.5)).astype(BF16)
    conv_chunk(1)
    k = seg(COL_K, SB_DIM)
    store_kv(k_ref, k)
    kb_ref[...] = k.astype(BF16)
    conv_chunk(2)
    v = seg(COL_V, SB_DIM)
    store_kv(v_ref, v)
    vb_ref[...] = v.astype(BF16)
    if fused_conv:
        last_rows = ext_ref[tm:tm + SUBLANES, :]
        tail_ref[0] = last_rows
        ext_ref[0:SUBLANES, :] = last_rows
    z_ref[...] = seg(COL_Z, GDN_DIM).astype(BF16)
    for j in range(2 * D_MODEL // SB_DIM):
        sg_ref[:, j * SB_DIM:(j + 1) * SB_DIM] = _sigmoid(seg(COL_GA + j * SB_DIM, SB_DIM)).astype(BF16)
    ab_ref[...] = seg(COL_AB, LANES)


def _inproj(x2d, norm_w, w_perm, tm, page=None, conv_w=None, seq_len=None):
    n = x2d.shape[0]
    row = lambda width: pl.BlockSpec((tm, width), lambda i: (i, 0))
    fused_conv = conv_w is not None
    if page is None:
        kv_shape, kv_spec = jax.ShapeDtypeStruct((n, SB_DIM), F32), row(SB_DIM)
    else:
        kv_shape = jax.ShapeDtypeStruct((n // page, SB_DIM, page), F32)
        kv_spec = pl.BlockSpec((tm // page, SB_DIM, page), lambda i: (i, 0, 0))
    out_shapes = (
        jax.ShapeDtypeStruct((n, SB_DIM), BF16),
        kv_shape,
        kv_shape,
        jax.ShapeDtypeStruct((n, SB_DIM), BF16),
        jax.ShapeDtypeStruct((n, SB_DIM), BF16),
        jax.ShapeDtypeStruct((n, CONV_DIM), F32),
        jax.ShapeDtypeStruct((n, GDN_DIM), BF16),
        jax.ShapeDtypeStruct((n, LANES), F32),
        jax.ShapeDtypeStruct((n, 2 * D_MODEL), BF16),
    )
    out_specs = tuple(kv_spec if i in (1, 2) else row(s.shape[1]) for i, s in enumerate(out_shapes))
    in_specs = [row(D_MODEL), _const_spec((1, D_MODEL)), _const_spec((D_MODEL, IN_COLS))]
    operands = (x2d, norm_w, w_perm)
    scratch = []
    if fused_conv:
        in_specs.append(_const_spec((CONV_WIDTH, CONV_DIM)))
        operands += (conv_w,)
        out_shapes += (jax.ShapeDtypeStruct((n // tm, SUBLANES, CONV_DIM), F32),)
        out_specs += (pl.BlockSpec((1, SUBLANES, CONV_DIM), lambda i: (i, 0, 0)),)
        scratch = [pltpu.VMEM((tm + SUBLANES, CONV_DIM), F32)]
    return pl.pallas_call(
        functools.partial(_inproj_kernel, page, seq_len // tm if fused_conv else None),
        grid=(n // tm,),
        in_specs=in_specs,
        out_specs=out_specs,
        out_shape=out_shapes,
        scratch_shapes=scratch,
        compiler_params=_params("arbitrary" if fused_conv else "parallel"),
        name="inproj",
    )(*operands)


def _sb_tile(z, carry, ntri, mask):
    sp = _softplus(z)
    if mask is not None:
        sp = jnp.where(mask, sp, 0.0)
    suffix = _dot(sp.astype(BF16), ntri)
    w = jnp.exp(z - sp + suffix + carry)
    if mask is not None:
        w = jnp.where(mask, w, 0.0)
    return w.astype(BF16), carry - jnp.sum(sp, axis=1, keepdims=True)


def _sb_prompt_kernel(bias_ref, q_ref, k_ref, v_ref, ntri_ref, o_ref,
                      qs_scr, bias_scr, z_scr, beta_scr, decay_scr, run_scr, acc_scr):
    hp = pl.program_id(1)
    tq, tk = SB_QBLOCK, SB_BLOCK
    ratio = tq // tk
    n_q = q_ref.shape[1] // tq
    n_tiles = ratio * n_q * (n_q + 1) // 2
    lane = lax.broadcasted_iota(jnp.int32, (1, LANES), 1)
    head_lanes = [lane < SB_HEAD_DIM, lane >= SB_HEAD_DIM]
    zero = jnp.zeros((), BF16)
    rows = [slice(j * tq, (j + 1) * tq) for j in range(2)]

    for ref in (z_scr, beta_scr, decay_scr, run_scr, acc_scr):
        ref[...] = jnp.zeros_like(ref)
    row = lax.broadcasted_iota(jnp.int32, (tq, tk), 0)
    col = lax.broadcasted_iota(jnp.int32, (tq, tk), 1)
    for j in range(2):
        bias = bias_ref[2 * hp + j]
        bias_scr[0, j] = jnp.full((tq, tk), bias, F32)
        for m in range(ratio):
            bias_scr[1 + m, j] = jnp.where(m * tk + col < row, bias, -1e30)
    for i in range(n_q):
        q = q_ref[0, i * tq:(i + 1) * tq, :]
        for j in range(2):
            qs_scr[i, rows[j], :] = jnp.where(head_lanes[j], q, zero)

    def key_block(ref, kb):
        return ref[0, pl.ds(pl.multiple_of(kb * tk, tk), tk), :]

    def step(s, tiles):
        (q_a, k_a), (q_b, k_b), (q_d, k_d) = tiles
        chunks = [slice(c * SB_ROW_CHUNK, (c + 1) * SB_ROW_CHUNK) for c in range(2 * tq // SB_ROW_CHUNK)]
        zs = [z_scr[r, :] for r in chunks]
        z_scr[...] = _dot_nt(qs_scr[q_a], key_block(k_ref, k_a))
        valid_d = jnp.logical_and(s >= 2, s - 2 < n_tiles)
        v = key_block(v_ref, k_d)
        w = jnp.concatenate([beta_scr[rows[j], :] * decay_scr[rows[j], :] for j in range(2)], axis=1)
        vm = jnp.concatenate(
            [jnp.where(jnp.logical_and(head_lanes[j], valid_d), v, zero) for j in range(2)], axis=0)
        pv = _dot(w, vm)
        acc_rows = pl.ds(pl.multiple_of(q_d * tq, tq), tq)
        acc_scr[acc_rows, :] = acc_scr[acc_rows, :] + pv
        first_of_block = k_b == ratio * q_b + ratio - 1
        offset = jnp.maximum(k_b - ratio * q_b + 1, 0)
        one = jnp.ones((), BF16)
        sps = []
        for r, z_raw in zip(chunks, zs):
            j, r_in_head = divmod(r.start, tq)
            z = z_raw + bias_scr[offset, j, r_in_head:r_in_head + SB_ROW_CHUNK, :]
            t = jnp.exp2(jnp.abs(z) * NEG_LOG2E)
            log1p_t = jnp.log(1.0 + t).astype(BF16)
            z16, t16 = z.astype(BF16), t.astype(BF16)
            sps.append(jnp.maximum(z16, zero) + log1p_t)
            beta_scr[r, :] = jnp.where(z16 >= zero, one, t16) / (one + t16)
        suffix = _dot(jnp.concatenate(sps, axis=0), ntri_ref[...])
        for r, sp in zip(chunks, sps):
            run = jnp.where(first_of_block, 0.0, run_scr[r, :])
            decay_scr[r, :] = jnp.exp(suffix[r, :] + run).astype(BF16)
            run_scr[r, :] = run + suffix[r, 0:1] - sp[:, 0:1].astype(F32)
        wrap = k_a == 0
        q_n = jnp.where(wrap, jnp.minimum(q_a + 1, n_q - 1), q_a)
        k_n = jnp.where(wrap, ratio * q_n + ratio - 1, k_a - 1)
        return ((q_n, k_n),) + tiles[:2]

    first = (jnp.int32(0), jnp.int32(ratio - 1))
    lax.fori_loop(0, n_tiles + 2, step, (first,) * 3)
    o_ref[0] = acc_scr[...].astype(BF16)


def _sb_prompt(sb_bias, q, k, v, ntri):
    bsz, t_len, _ = q.shape
    tq, tk = SB_QBLOCK, SB_BLOCK
    seq = pl.BlockSpec((1, t_len, LANES), lambda b, h, *_: (b, 0, h))
    return pl.pallas_call(
        _sb_prompt_kernel,
        grid_spec=pltpu.PrefetchScalarGridSpec(
            num_scalar_prefetch=1,
            grid=(bsz, SB_DIM // LANES),
            in_specs=[seq, seq, seq, pl.BlockSpec((tk, tk), lambda b, h, *_: (0, 0))],
            out_specs=seq,
            scratch_shapes=[
                pltpu.VMEM((t_len // tq, 2 * tq, LANES), BF16),
                pltpu.VMEM((1 + tq // tk, 2, tq, tk), F32),
                pltpu.VMEM((2 * tq, tk), F32),
                pltpu.VMEM((2 * tq, tk), BF16),
                pltpu.VMEM((2 * tq, tk), BF16),
                pltpu.VMEM((2 * tq, 1), F32),
                pltpu.VMEM((t_len, LANES), F32),
            ],
        ),
        out_shape=jax.ShapeDtypeStruct((bsz, t_len, SB_DIM), BF16),
        compiler_params=_params("parallel", "parallel"),
        name="sb_prompt",
    )(sb_bias, q, k, v, ntri)


def _sb_decode_kernel(n_pages, t_new, pt_ref, bias_ref, q_ref, kn_ref, vn_ref, ntri_ref, *refs):
    kt_pages = refs[:n_pages]
    vt_pages = refs[n_pages:2 * n_pages]
    o_ref = refs[2 * n_pages]
    page = kt_pages[0].shape[2]
    rows = t_new * SB_HEADS
    r_iota = lax.broadcasted_iota(jnp.int32, (rows, SB_DIM), 0)
    l_iota = lax.broadcasted_iota(jnp.int32, (rows, SB_DIM), 1)
    head_of_row = r_iota % SB_HEADS
    own_lanes = (l_iota // SB_HEAD_DIM) == head_of_row

    q = q_ref[0].astype(F32)
    q_rep = jnp.broadcast_to(q[:, None, :], (t_new, SB_HEADS, SB_DIM)).reshape(rows, SB_DIM)
    q_bd = jnp.where(own_lanes, q_rep, 0.0).astype(BF16)

    r1 = lax.broadcasted_iota(jnp.int32, (rows, 1), 0)
    bias = jnp.zeros((rows, 1), F32)
    for h in range(SB_HEADS):
        bias = jnp.where(r1 % SB_HEADS == h, bias_ref[h], bias)

    pad = jnp.zeros((page - SUBLANES, SB_DIM), F32)
    k_new = jnp.concatenate([kn_ref[0], pad], axis=0).astype(BF16)
    v_new = jnp.concatenate([vn_ref[0], pad], axis=0).astype(BF16)
    kcol = lax.broadcasted_iota(jnp.int32, (rows, page), 1)
    qtok = lax.broadcasted_iota(jnp.int32, (rows, page), 0) // SB_HEADS
    z = _dot_nt(q_bd, k_new) + bias
    w, carry = _sb_tile(z, jnp.zeros((rows, 1), F32), ntri_ref[:page, :page], kcol < qtok)
    acc = _dot(w, v_new)

    blk = ntri_ref.shape[0] // page
    order = range(n_pages // blk - 1, -1, -1)
    lane_cat = lambda pages, pb: jnp.concatenate(
        [pages[pb * blk + i][0] for i in range(blk)], axis=1).astype(BF16)
    zs = [_dot(q_bd, lane_cat(kt_pages, pb)) + bias for pb in order]
    sps = [_softplus(z) for z in zs]
    suffixes = [_dot(sp.astype(BF16), ntri_ref[...]) for sp in sps]
    for pb, z, sp, suffix in zip(order, zs, sps, suffixes):
        w = jnp.exp(z - sp + suffix + carry).astype(BF16)
        carry = carry - jnp.sum(sp, axis=1, keepdims=True)
        acc = acc + _dot_nt(w, lane_cat(vt_pages, pb))

    acc = jnp.where(own_lanes, acc, 0.0)
    o_ref[0] = jnp.sum(acc.reshape(t_new, SB_HEADS, SB_DIM), axis=1).astype(BF16)


def _sb_decode(page_table, sb_bias, q, k_new8, v_new8, ntri, cache_kt, cache_vt):
    n_seq, t_new, _ = q.shape
    n_pages = page_table.shape[1]
    page = cache_kt.shape[2]

    def page_spec(p):
        return pl.BlockSpec((1, SB_DIM, page), lambda s, pt, b: (pt[s, p], 0, 0))

    seq_spec = lambda r: pl.BlockSpec((1, r, SB_DIM), lambda s, pt, b: (s, 0, 0))
    return pl.pallas_call(
        functools.partial(_sb_decode_kernel, n_pages, t_new),
        grid_spec=pltpu.PrefetchScalarGridSpec(
            num_scalar_prefetch=2,
            grid=(n_seq,),
            in_specs=[seq_spec(t_new), seq_spec(SUBLANES), seq_spec(SUBLANES),
                      pl.BlockSpec(ntri.shape, lambda s, pt, b: (0, 0))]
                     + [page_spec(p) for p in range(n_pages)] * 2,
            out_specs=seq_spec(t_new),
        ),
        out_shape=jax.ShapeDtypeStruct((n_seq, t_new, SB_DIM), BF16),
        compiler_params=_params("parallel"),
        name="sb_decode",
    )(page_table, sb_bias, q, k_new8, v_new8, ntri, *([cache_kt] * n_pages), *([cache_vt] * n_pages))


def _conv_silu(ext_ref, first, rows, cw_ref, cols=slice(None)):
    acc = None
    for w in range(CONV_WIDTH):
        term = ext_ref[first + w:first + w + rows, cols] * cw_ref[w:w + 1, cols]
        acc = term if acc is None else acc + term
    return _silu(acc)


def _l2norm(x):
    return x * lax.rsqrt(jnp.sum(x * x, axis=-1, keepdims=True) + L2_EPS)


def _gate_terms(ab, alog_ref, dtb_ref):
    g = -jnp.exp(alog_ref[...]) * _softplus(ab + dtb_ref[...])
    return g, _sigmoid(ab)


def _gated_out_norm(o, z, gw):
    o = o * lax.rsqrt(jnp.mean(o * o, axis=-1, keepdims=True) + NORM_EPS) * gw
    return o * _silu(z)


def _gdn_prep_kernel(conv_ref, ab_ref, alog_ref, dtb_ref, lincl_ref, hones_ref,
                     un_ref, wn_ref, qg_ref, kg_ref, att_ref, cg_ref):
    rows = conv_ref.shape[1]
    c_len = GDN_CHUNK
    dk = GDN_HEAD_DIM
    conv = conv_ref[0]

    g_all, beta_all = _gate_terms(ab_ref[0], alog_ref, dtb_ref)
    r_i = lax.broadcasted_iota(jnp.int32, (c_len, c_len), 0)
    c_i = lax.broadcasted_iota(jnp.int32, (c_len, c_len), 1)
    incl = c_i <= r_i
    strict = c_i < r_i
    lincl = lincl_ref[...]
    n_chunks = rows // c_len

    cg_chunks = [_dot_exact01(lincl, g_all[c * c_len:(c + 1) * c_len, :]) for c in range(n_chunks)]
    cg_t_pairs = []
    for pair in range(n_chunks // 2):
        cg_pair = jnp.concatenate(cg_chunks[2 * pair:2 * pair + 2], axis=0)
        cg_ref[0, 2 * pair * c_len:(2 * pair + 2) * c_len, :] = cg_pair
        cg_t_pairs.append(cg_pair.T)

    problems = [(c, h) for c in range(n_chunks) for h in range(GDN_HEADS)]

    def head_cols(base, h):
        return slice(base + h * dk, base + (h + 1) * dk)

    def l2norm_heads(x):
        sq = x * x
        hi = sq.astype(BF16)
        lo = (sq - hi.astype(F32)).astype(BF16)
        return x * lax.rsqrt(_dot(hi, hones_ref[...]) + _dot(lo, hones_ref[...]) + L2_EPS)

    q_all = l2norm_heads(conv[:, :GDN_DIM]) * (dk ** -0.5)
    k_all = l2norm_heads(conv[:, GDN_DIM:2 * GDN_DIM])

    qs, ks, vs, bcs, cgcols, decays = [], [], [], [], [], []
    for c, h in problems:
        r = slice(c * c_len, (c + 1) * c_len)
        qs.append(q_all[r, head_cols(0, h)])
        ks.append(k_all[r, head_cols(0, h)])
        vs.append(conv[r, head_cols(2 * GDN_DIM, h)])
        bcs.append(beta_all[r, GDN_HEADS + h:GDN_HEADS + h + 1])
        cg_col = cg_chunks[c][:, h:h + 1]
        cg_row = cg_t_pairs[c // 2][h:h + 1, (c % 2) * c_len:(c % 2 + 1) * c_len]
        cgcols.append(cg_col)
        decays.append(jnp.exp(jnp.where(incl, cg_col - cg_row, -jnp.inf)))
    kbs = [k.astype(BF16) for k in ks]
    kks = [_dot_nt(kb, kb) for kb in kbs]

    powers = [jnp.where(strict, -(bc * kk * dec), 0.0) for bc, kk, dec in zip(bcs, kks, decays)]
    inv_off = powers
    for _ in range(int(math.log2(c_len)) - 1):
        pbs = [p.astype(BF16) for p in powers]
        powers = [_dot(pb, pb) for pb in pbs]
        next_pbs = [p.astype(BF16) for p in powers]
        inv_off = [t + p + _dot(t.astype(BF16), pb) for t, p, pb in zip(inv_off, powers, next_pbs)]

    for (c, h), q, k, v, bc, cg_col, dec, t_off in zip(problems, qs, ks, vs, bcs, cgcols, decays, inv_off):
        r = slice(c * c_len, (c + 1) * c_len)
        cols = head_cols(0, h)
        rhs = jnp.concatenate([v * bc, k * (bc * jnp.exp(cg_col))], axis=1)
        sol = rhs + _dot(t_off.astype(BF16), rhs.astype(BF16))
        un_ref[0, r, cols] = sol[:, :dk]
        wn_ref[0, r, cols] = sol[:, dk:].astype(BF16)
        qg_ref[0, r, cols] = (q * jnp.exp(cg_col)).astype(BF16)
        g_last = cg_chunks[c][c_len - 1:c_len, h:h + 1]
        kg_ref[0, r, cols] = (k * jnp.exp(g_last - cg_col)).astype(BF16)
        qk = _dot_nt(q.astype(BF16), k.astype(BF16))
        att_ref[0, h, r, :] = (qk * dec).astype(BF16)


def _gdn_prep(conv, ab, alog_pad, dtb_pad, lincl):
    bsz, t_len, _ = conv.shape
    head_of = jnp.arange(GDN_DIM) // GDN_HEAD_DIM
    head_ones = (head_of[:, None] == head_of[None, :]).astype(BF16)
    rows = min(PREP_ROWS, t_len)
    blk = lambda width: pl.BlockSpec((1, rows, width), lambda b, i: (b, i, 0))
    out_shapes = (
        jax.ShapeDtypeStruct((bsz, t_len, GDN_DIM), F32),
        jax.ShapeDtypeStruct((bsz, t_len, GDN_DIM), BF16),
        jax.ShapeDtypeStruct((bsz, t_len, GDN_DIM), BF16),
        jax.ShapeDtypeStruct((bsz, t_len, GDN_DIM), BF16),
        jax.ShapeDtypeStruct((bsz, GDN_HEADS, t_len, GDN_CHUNK), BF16),
        jax.ShapeDtypeStruct((bsz, t_len, LANES), F32),
    )
    return pl.pallas_call(
        _gdn_prep_kernel,
        grid=(bsz, t_len // rows),
        in_specs=[
            blk(CONV_DIM),
            blk(LANES),
            _const_spec((1, LANES)),
            _const_spec((1, LANES)),
            _const_spec((GDN_CHUNK, GDN_CHUNK)),
            _const_spec((GDN_DIM, GDN_DIM)),
        ],
        out_specs=(blk(GDN_DIM), blk(GDN_DIM), blk(GDN_DIM), blk(GDN_DIM),
                   pl.BlockSpec((1, GDN_HEADS, rows, GDN_CHUNK), lambda b, i: (b, 0, i, 0)),
                   blk(LANES)),
        out_shape=out_shapes,
        compiler_params=_params("parallel", "parallel"),
        name="gdn_prep",
    )(conv, ab, alog_pad, dtb_pad, lincl, head_ones)


def _gdn_scan_kernel(un_ref, wn_ref, qg_ref, kg_ref, att_ref, cg_ref, z_ref, gw_ref, o_ref, s_out_ref, s_ref):
    c = pl.program_id(0)
    bsz = un_ref.shape[0]
    c_len = GDN_CHUNK
    dk = GDN_HEAD_DIM

    @pl.when(c == 0)
    def _():
        s_ref[...] = jnp.zeros_like(s_ref)

    chains = [(b, h, slice(h * dk, (h + 1) * dk)) for b in range(bsz) for h in range(GDN_HEADS)]
    states = [s_ref[b, h] for b, h, _ in chains]
    for sub in range(un_ref.shape[1] // c_len):
        r = slice(sub * c_len, (sub + 1) * c_len)
        decay_last = [jnp.exp(cg_ref[b, r.stop - 1:r.stop, :]) for b in range(bsz)]
        sbs = [s.astype(BF16) for s in states]
        ws = [_dot(wn_ref[b, r, cols], sb) for (b, _, cols), sb in zip(chains, sbs)]
        qs = [_dot(qg_ref[b, r, cols], sb) for (b, _, cols), sb in zip(chains, sbs)]
        vbs = [(un_ref[b, r, cols] - w).astype(BF16) for (b, _, cols), w in zip(chains, ws)]
        new_states = []
        for (b, h, cols), s, q_s, vb in zip(chains, states, qs, vbs):
            o = q_s + _dot(att_ref[b, h, r, :], vb)
            new_states.append(s * decay_last[b][:, h:h + 1] + _dot_tn(kg_ref[b, r, cols], vb))
            o_ref[b, r, cols] = _gated_out_norm(o, z_ref[b, r, cols].astype(F32), gw_ref[...]).astype(BF16)
        states = new_states
    for (b, h, _), s in zip(chains, states):
        s_ref[b, h] = s

    @pl.when(c == pl.num_programs(0) - 1)
    def _():
        s_out_ref[...] = s_ref[...]


def _gdn_scan(un, wn, qg, kg, att, cg, z, gw):
    bsz, t_len, _ = un.shape
    rows = min(SCAN_ROWS, t_len)
    blk = lambda width: pl.BlockSpec((bsz, rows, width), lambda c: (0, c, 0))
    state_shape = (bsz, GDN_HEADS, GDN_HEAD_DIM, GDN_HEAD_DIM)
    return pl.pallas_call(
        _gdn_scan_kernel,
        grid=(t_len // rows,),
        in_specs=[blk(GDN_DIM), blk(GDN_DIM), blk(GDN_DIM), blk(GDN_DIM),
                  pl.BlockSpec((bsz, GDN_HEADS, rows, GDN_CHUNK), lambda c: (0, 0, c, 0)),
                  blk(LANES), blk(GDN_DIM), _const_spec((1, GDN_HEAD_DIM))],
        out_specs=(blk(GDN_DIM), pl.BlockSpec(state_shape, lambda c: (0, 0, 0, 0))),
        out_shape=(jax.ShapeDtypeStruct((bsz, t_len, GDN_DIM), BF16),
                   jax.ShapeDtypeStruct(state_shape, F32)),
        scratch_shapes=[pltpu.VMEM(state_shape, F32)],
        compiler_params=_params("arbitrary"),
        name="gdn_scan",
    )(un, wn, qg, kg, att, cg, z, gw)


def _gdn_decode_kernel(t_new, u_ref, hist_ref, ab_ref, z_ref, s_in_ref, cw_ref, alog_ref, dtb_ref, gw_ref,
                       o_ref, s_out_ref, ext_ref, ab_scr):
    dk = GDN_HEAD_DIM
    n_hist = CONV_WIDTH - 1
    n_seq = u_ref.shape[0]
    tile = (SUBLANES, LANES)
    tail = jnp.zeros((LANES - SUBLANES, LANES), F32)
    pad_rows = lambda x: jnp.concatenate([x, tail], axis=0)
    row = lax.broadcasted_iota(jnp.int32, tile, 0)
    lane = lax.broadcasted_iota(jnp.int32, tile, 1)
    real = row < t_new
    ext_ref[...] = jnp.zeros_like(ext_ref)
    ab_scr[...] = jnp.zeros_like(ab_scr)

    problems = [(g, h) for g in range(n_seq) for h in range(GDN_HEADS)]
    q8, k8, v8, e_col, b_col, w_col, decay, e_last = {}, {}, {}, {}, {}, {}, {}, {}
    for g in range(n_seq):
        ext_ref[g, 0:n_hist, :] = hist_ref[g]
        ext_ref[g, n_hist:n_hist + t_new, :] = u_ref[g]
        ab_scr[g, 0:t_new, :] = ab_ref[g]
        conv = _conv_silu(ext_ref.at[g], 0, SUBLANES, cw_ref)
        g_all, beta_all = _gate_terms(ab_scr[g], alog_ref, dtb_ref)
        cg = g_all
        for shift in range(1, t_new):
            cg = cg + jnp.where(row >= shift, pltpu.roll(g_all, shift, axis=0), 0.0)
        cg_last = cg[t_new - 1:t_new, :]
        e_all = jnp.exp(cg)
        w_all = jnp.exp(cg_last - cg)
        cg_t = pad_rows(cg).T
        for h in range(GDN_HEADS):
            p = (g, h)
            q8[p] = _l2norm(conv[:, h * dk:(h + 1) * dk]) * (dk ** -0.5)
            k8[p] = _l2norm(conv[:, GDN_DIM + h * dk:GDN_DIM + (h + 1) * dk])
            v8[p] = conv[:, 2 * GDN_DIM + h * dk:2 * GDN_DIM + (h + 1) * dk]
            e_col[p] = e_all[:, h:h + 1]
            b_col[p] = beta_all[:, GDN_HEADS + h:GDN_HEADS + h + 1]
            w_col[p] = w_all[:, h:h + 1]
            e_last[p] = e_all[t_new - 1:t_new, h:h + 1]
            decay[p] = jnp.exp(jnp.where(lane <= row, cg[:, h:h + 1] - cg_t[h:h + 1, :], -jnp.inf))

    states = {p: s_in_ref[p[0], p[1]] for p in problems}
    kq = {p: jnp.concatenate([k8[p], q8[p]], axis=0).astype(BF16) for p in problems}
    ks0_qs0 = {p: _dot(kq[p], states[p].astype(BF16)) for p in problems}
    gram = {p: _dot_nt(kq[p], pad_rows(k8[p]).astype(BF16)) for p in problems}
    deltas = {}
    for p in problems:
        m = b_col[p] * jnp.where(lane < row, gram[p][:SUBLANES] * decay[p], 0.0)
        r = b_col[p] * (v8[p] - e_col[p] * ks0_qs0[p][:SUBLANES])
        for j in range(t_new - 1):
            r = r - m[:, j:j + 1] * r[j:j + 1, :]
        deltas[p] = jnp.where(real, r, 0.0)
    updates = {}
    for p in problems:
        kw_t = pad_rows(jnp.where(real, k8[p] * w_col[p], 0.0)).T
        updates[p] = _dot(kw_t.astype(BF16), pad_rows(deltas[p]).astype(BF16))
    for g, h in problems:
        p = (g, h)
        cols = slice(h * dk, (h + 1) * dk)
        coef = gram[p][SUBLANES:] * decay[p]
        o = e_col[p] * ks0_qs0[p][SUBLANES:]
        for j in range(t_new):
            o = o + coef[:, j:j + 1] * deltas[p][j:j + 1, :]
        o = _gated_out_norm(o[:t_new], z_ref[g, :, cols].astype(F32), gw_ref[...])
        o_ref[g, :, cols] = o.astype(BF16)
        s_out_ref[g, h] = states[p] * e_last[p] + updates[p]


def _gdn_decode(u, hist, ab, z, s_in, conv_w, alog_pad, dtb_pad, gw):
    n_seq, t_new, _ = u.shape
    grp = DEC_GROUP
    seq = lambda r, width: pl.BlockSpec((grp, r, width), lambda i: (i, 0, 0))
    state = pl.BlockSpec((grp, GDN_HEADS, GDN_HEAD_DIM, GDN_HEAD_DIM), lambda i: (i, 0, 0, 0))
    return pl.pallas_call(
        functools.partial(_gdn_decode_kernel, t_new),
        grid=(n_seq // grp,),
        in_specs=[seq(t_new, CONV_DIM), seq(CONV_WIDTH - 1, CONV_DIM), seq(t_new, LANES), seq(t_new, GDN_DIM),
                  state, _const_spec((CONV_WIDTH, CONV_DIM)), _const_spec((1, LANES)), _const_spec((1, LANES)),
                  _const_spec((1, GDN_HEAD_DIM))],
        out_specs=(seq(t_new, GDN_DIM), state),
        out_shape=(jax.ShapeDtypeStruct((n_seq, t_new, GDN_DIM), BF16),
                   jax.ShapeDtypeStruct(s_in.shape, F32)),
        scratch_shapes=[pltpu.VMEM((grp, 2 * SUBLANES, CONV_DIM), F32), pltpu.VMEM((grp, SUBLANES, LANES), F32)],
        compiler_params=_params("parallel"),
        name="gdn_decode",
    )(u, hist, ab, z, s_in, conv_w, alog_pad, dtb_pad, gw)


def _tail_kernel(x_ref, oa_ref, ob_ref, sg_ref, wpa_ref, wpb_ref, wo_ref, nw_ref, wup_ref, wdown_ref, nf_ref,
                 y_ref, acc_ref, hm_ref):
    f = pl.program_id(1)

    @pl.when(f == 0)
    def _():
        for r0 in range(0, x_ref.shape[0], MERGE_ROWS):
            r = slice(r0, min(r0 + MERGE_ROWS, x_ref.shape[0]))
            y_a = _dot(oa_ref[r, :], wpa_ref[...])
            y_b = _dot(ob_ref[r, :], wpb_ref[...])
            mix = sg_ref[r, :D_MODEL] * y_a + sg_ref[r, D_MODEL:] * y_b
            x1 = x_ref[r, :] + _dot(mix.astype(BF16), wo_ref[...])
            acc_ref[r, :] = x1
            hm_ref[r, :] = _rmsnorm(x1, nw_ref[...]).astype(BF16)

    up = jnp.maximum(_dot(hm_ref[...], wup_ref[...]), 0.0)
    acc_ref[...] += _dot((up * up).astype(BF16), wdown_ref[...])

    @pl.when(f == pl.num_programs(1) - 1)
    def _():
        y_ref[...] = _rmsnorm(acc_ref[...], nf_ref[...])


def _tail(x2d, o_a, o_b, sg, lw, norm_f, tm, tf):
    n = x2d.shape[0]
    row = lambda width: pl.BlockSpec((tm, width), lambda i, f: (i, 0))
    return pl.pallas_call(
        _tail_kernel,
        grid=(n // tm, D_FF // tf),
        in_specs=[row(D_MODEL), row(SB_DIM), row(GDN_DIM), row(2 * D_MODEL),
                  _const_spec(lw["w_pa"].shape), _const_spec(lw["w_pb"].shape), _const_spec(lw["w_o"].shape),
                  _const_spec((1, D_MODEL)),
                  pl.BlockSpec((D_MODEL, tf), lambda i, f: (0, f)),
                  pl.BlockSpec((tf, D_MODEL), lambda i, f: (f, 0)),
                  _const_spec((1, D_MODEL))],
        out_specs=row(D_MODEL),
        out_shape=jax.ShapeDtypeStruct((n, D_MODEL), F32),
        scratch_shapes=[pltpu.VMEM((tm, D_MODEL), F32), pltpu.VMEM((tm, D_MODEL), BF16)],
        compiler_params=_params("parallel", "arbitrary"),
        name="tail",
    )(x2d, o_a, o_b, sg, lw["w_pa"], lw["w_pb"], lw["w_o"], lw["norm_mlp_w"], lw["w_up"], lw["w_down"], norm_f)


def _permute_w_in(w):
    n_main = 3 * SB_DIM + CONV_DIM + GDN_DIM
    gates = w[:, n_main + 2 * GDN_HEADS:]
    ab = w[:, n_main:n_main + 2 * GDN_HEADS]
    pad = jnp.zeros((w.shape[0], LANES - 2 * GDN_HEADS), w.dtype)
    return jnp.concatenate([w[:, :n_main], gates, ab, pad], axis=1).astype(BF16)


def _lane_pad(vec, offset):
    return jnp.zeros((1, LANES), F32).at[0, offset:offset + vec.shape[0]].set(vec.astype(F32))


def _layer_common(x2d, lw, tm, **kwargs):
    return _inproj(x2d, lw["norm_mix_w"], lw["w_in"], tm, **kwargs)


def kernel(x_prompt, x_sample, cache_k, cache_v, page_table, state_conv, state_ssm, norm_mix_w, w_in, sb_bias,
           conv_w, a_log, dt_bias, gdn_norm_w, w_pa, w_pb, w_o, norm_mlp_w, w_up, w_down, norm_final_w):
    depth = w_in.shape[0]
    assert depth == 1, "the residual stream is normalised once, after the only layer"
    b_p, t_p, _ = x_prompt.shape
    b_s, t_s, _ = x_sample.shape
    n_pool, page = cache_k.shape[1], cache_k.shape[2]
    l = 0
    lw = {
        "norm_mix_w": norm_mix_w[l].reshape(1, D_MODEL).astype(F32),
        "w_in": _permute_w_in(w_in[l]),
        "w_pa": w_pa[l].astype(BF16), "w_pb": w_pb[l].astype(BF16), "w_o": w_o[l].astype(BF16),
        "norm_mlp_w": norm_mlp_w[l].reshape(1, D_MODEL).astype(F32),
        "w_up": w_up[l].astype(BF16), "w_down": w_down[l].astype(BF16),
    }
    norm_f = norm_final_w.reshape(1, D_MODEL).astype(F32)
    bias = sb_bias[l].astype(F32)
    cw = conv_w[l].astype(F32)
    alog_pad = _lane_pad(a_log[l], 0)
    dtb_pad = _lane_pad(dt_bias[l], 0)
    gw = gdn_norm_w[l].reshape(1, GDN_HEAD_DIM).astype(F32)
    ntri = -jnp.tril(jnp.ones((SB_BLOCK, SB_BLOCK), BF16), -1)
    lincl = jnp.tril(jnp.ones((GDN_CHUNK, GDN_CHUNK), BF16))

    xp = x_prompt.reshape(b_p * t_p, D_MODEL).astype(F32)
    tm_p = 256
    q, k_t, v_t, kb, vb, conv, z, ab, sg, u_tail = _layer_common(xp, lw, tm_p, page=page, conv_w=cw, seq_len=t_p)
    shp = lambda a: a.reshape(b_p, t_p, a.shape[-1])
    o_a = _sb_prompt(bias, shp(q), shp(kb), shp(vb), ntri)
    un, wn, qg, kg, att, cg = _gdn_prep(shp(conv), shp(ab), alog_pad, dtb_pad, lincl)
    o_b, ssm_p = _gdn_scan(un, wn, qg, kg, att, cg, shp(z), gw)
    y_p = _tail(xp, o_a.reshape(-1, SB_DIM), o_b.reshape(-1, GDN_DIM), sg, lw, norm_f, 1024, 1024)
    y_prompt = y_p.reshape(b_p, t_p, D_MODEL).astype(x_prompt.dtype)
    as_pages = lambda a_t: jnp.transpose(
        a_t.reshape(b_p, t_p // page, SB_HEADS, SB_HEAD_DIM, page), (0, 1, 4, 2, 3))[None]
    new_k_prompt = as_pages(k_t).astype(cache_k.dtype)
    new_v_prompt = as_pages(v_t).astype(cache_v.dtype)
    seq_tails = u_tail.reshape(b_p, t_p // tm_p, SUBLANES, CONV_DIM)[:, -1]
    new_conv_prompt = seq_tails[None, :, SUBLANES - (CONV_WIDTH - 1):, :].astype(state_conv.dtype)
    new_ssm_prompt = ssm_p[None].astype(state_ssm.dtype)

    xs = x_sample.reshape(b_s * t_s, D_MODEL).astype(F32)
    tm_s = min(256, b_s * t_s)
    q, k, v, _, _, u, z, ab, sg = _layer_common(xs, lw, tm_s)
    shs = lambda a: a.reshape(b_s, t_s, a.shape[-1])
    pad8 = lambda a: jnp.pad(shs(a), ((0, 0), (0, SUBLANES - t_s), (0, 0)))
    pages_t = lambda c: jnp.transpose(c[l], (0, 2, 3, 1)).reshape(n_pool, SB_DIM, page).astype(F32)
    o_a = _sb_decode(page_table, bias, shs(q), pad8(k), pad8(v), ntri, pages_t(cache_k), pages_t(cache_v))
    o_b, ssm_s = _gdn_decode(shs(u), state_conv[l].astype(F32), shs(ab), shs(z), state_ssm[l].astype(F32),
                             cw, alog_pad, dtb_pad, gw)
    y_s = _tail(xs, o_a.reshape(-1, SB_DIM), o_b.reshape(-1, GDN_DIM), sg, lw, norm_f, tm_s, 512)
    y_sample = y_s.reshape(b_s, t_s, D_MODEL).astype(x_sample.dtype)
    head_shape = (depth, b_s, t_s, SB_HEADS, SB_HEAD_DIM)
    new_k_sample = k.reshape(head_shape).astype(cache_k.dtype)
    new_v_sample = v.reshape(head_shape).astype(cache_v.dtype)
    new_conv_sample = jnp.concatenate([state_conv[l].astype(F32), shs(u)], axis=1)[None, :, t_s:, :].astype(state_conv.dtype)
    new_ssm_sample = ssm_s[None].astype(state_ssm.dtype)

    return (y_prompt, y_sample, new_k_prompt, new_v_prompt, new_k_sample, new_v_sample,
            new_conv_prompt, new_conv_sample, new_ssm_prompt, new_ssm_sample)
```

```python
import functools
import math

import jax
import jax.numpy as jnp
from jax import lax
from jax.experimental import pallas as pl
from jax.experimental.pallas import tpu as pltpu

F32 = jnp.float32
BF16 = jnp.bfloat16

D_MODEL = 1024
SB_HEADS = 8
SB_HEAD_DIM = 64
SB_DIM = SB_HEADS * SB_HEAD_DIM
GDN_HEADS = 4
GDN_HEAD_DIM = 128
GDN_DIM = GDN_HEADS * GDN_HEAD_DIM
CONV_WIDTH = 4
CONV_DIM = 3 * GDN_DIM
GDN_CHUNK = 64
D_FF = 4 * D_MODEL
NORM_EPS = 1e-6
L2_EPS = 1e-6
NEG_LOG2E = -1.4426950408889634

LANES = 128
SUBLANES = 8
VMEM_LIMIT_BYTES = 56 * 1024 * 1024

COL_Q = 0
COL_K = COL_Q + SB_DIM
COL_V = COL_K + SB_DIM
COL_U = COL_V + SB_DIM
COL_Z = COL_U + CONV_DIM
COL_GA = COL_Z + GDN_DIM
COL_GB = COL_GA + D_MODEL
COL_AB = COL_GB + D_MODEL
IN_COLS = COL_AB + LANES

SB_BLOCK = 256
SB_QBLOCK = 512
SB_ROW_CHUNK = 64
PREP_ROWS = 512
SCAN_ROWS = 256
MERGE_ROWS = 512
DEC_GROUP = 8


def _dot(a, b):
    return jnp.dot(a, b, preferred_element_type=F32)


def _dot_nt(a, b):
    return lax.dot_general(a, b, (((1,), (1,)), ((), ())), preferred_element_type=F32)


def _dot_tn(a, b):
    return lax.dot_general(a, b, (((0,), (0,)), ((), ())), preferred_element_type=F32)


def _dot_exact01(m01, x):
    h = x.astype(BF16)
    r = x - h.astype(F32)
    m = r.astype(BF16)
    l = (r - m.astype(F32)).astype(BF16)
    return _dot(m01, h) + (_dot(m01, m) + _dot(m01, l))


def _softplus(z):
    return jnp.maximum(z, 0.0) + jnp.log(1.0 + jnp.exp2(jnp.abs(z) * NEG_LOG2E))


def _sigmoid(z):
    return 1.0 / (1.0 + jnp.exp(-z))


def _silu(z):
    return z * _sigmoid(z)


def _rmsnorm(x, w):
    return x * lax.rsqrt(jnp.mean(x * x, axis=-1, keepdims=True) + NORM_EPS) * w


def _const_spec(shape):
    nd = len(shape)
    return pl.BlockSpec(shape, lambda *_: (0,) * nd, pipeline_mode=pl.Buffered(1))


def _params(*sem):
    return pltpu.CompilerParams(dimension_semantics=sem, vmem_limit_bytes=VMEM_LIMIT_BYTES)


def _inproj_kernel(page, tiles_per_seq, x_ref, nw_ref, w_ref, *refs):
    if tiles_per_seq is None:
        q_ref, k_ref, v_ref, kb_ref, vb_ref, u_ref, z_ref, ab_ref, sg_ref = refs
    else:
        cw_ref, q_ref, k_ref, v_ref, kb_ref, vb_ref, u_ref, z_ref, ab_ref, sg_ref, tail_ref, ext_ref = refs
    x = x_ref[...]
    hb = _rmsnorm(x, nw_ref[...]).astype(BF16)

    def seg(lo, width):
        return _dot(hb, w_ref[:, lo:lo + width])

    def store_kv(ref, val):
        if page is None:
            ref[...] = val
        else:
            val_t = val.T
            for p in range(val.shape[0] // page):
                ref[p] = val_t[:, p * page:(p + 1) * page]

    fused_conv = tiles_per_seq is not None
    u_cols = [slice(j * SB_DIM, (j + 1) * SB_DIM) for j in range(CONV_DIM // SB_DIM)]
    if fused_conv:
        tm = x.shape[0]

        @pl.when(pl.program_id(0) % tiles_per_seq == 0)
        def _():
            ext_ref[0:SUBLANES, :] = jnp.zeros((SUBLANES, CONV_DIM), F32)

        for j, cols in enumerate(u_cols):
            ext_ref[SUBLANES:, cols] = seg(COL_U + j * SB_DIM, SB_DIM)
    else:
        for j, cols in enumerate(u_cols):
            u_ref[:, cols] = seg(COL_U + j * SB_DIM, SB_DIM)

    def conv_chunk(j):
        if fused_conv:
            u_ref[:, u_cols[j]] = _conv_silu(ext_ref, SUBLANES - (CONV_WIDTH - 1), tm, cw_ref, u_cols[j])

    conv_chunk(0)
    q_ref[...] = (seg(COL_Q, SB_DIM) * (SB_HEAD_DIM ** -0.5)).astype(BF16)
    conv_chunk(1)
    k = seg(COL_K, SB_DIM)
    store_kv(k_ref, k)
    kb_ref[...] = k.astype(BF16)
    conv_chunk(2)
    v = seg(COL_V, SB_DIM)
    store_kv(v_ref, v)
    vb_ref[...] = v.astype(BF16)
    if fused_conv:
        last_rows = ext_ref[tm:tm + SUBLANES, :]
        tail_ref[0] = last_rows
        ext_ref[0:SUBLANES, :] = last_rows
    z_ref[...] = seg(COL_Z, GDN_DIM).astype(BF16)
    for j in range(2 * D_MODEL // SB_DIM):
        sg_ref[:, j * SB_DIM:(j + 1) * SB_DIM] = _sigmoid(seg(COL_GA + j * SB_DIM, SB_DIM)).astype(BF16)
    ab_ref[...] = seg(COL_AB, LANES)


def _inproj(x2d, norm_w, w_perm, tm, page=None, conv_w=None, seq_len=None):
    n = x2d.shape[0]
    row = lambda width: pl.BlockSpec((tm, width), lambda i: (i, 0))
    fused_conv = conv_w is not None
    if page is None:
        kv_shape, kv_spec = jax.ShapeDtypeStruct((n, SB_DIM), F32), row(SB_DIM)
    else:
        kv_shape = jax.ShapeDtypeStruct((n // page, SB_DIM, page), F32)
        kv_spec = pl.BlockSpec((tm // page, SB_DIM, page), lambda i: (i, 0, 0))
    out_shapes = (
        jax.ShapeDtypeStruct((n, SB_DIM), BF16),
        kv_shape,
        kv_shape,
        jax.ShapeDtypeStruct((n, SB_DIM), BF16),
        jax.ShapeDtypeStruct((n, SB_DIM), BF16),
        jax.ShapeDtypeStruct((n, CONV_DIM), F32),
        jax.ShapeDtypeStruct((n, GDN_DIM), BF16),
        jax.ShapeDtypeStruct((n, LANES), F32),
        jax.ShapeDtypeStruct((n, 2 * D_MODEL), BF16),
    )
    out_specs = tuple(kv_spec if i in (1, 2) else row(s.shape[1]) for i, s in enumerate(out_shapes))
    in_specs = [row(D_MODEL), _const_spec((1, D_MODEL)), _const_spec((D_MODEL, IN_COLS))]
    operands = (x2d, norm_w, w_perm)
    scratch = []
    if fused_conv:
        in_specs.append(_const_spec((CONV_WIDTH, CONV_DIM)))
        operands += (conv_w,)
        out_shapes += (jax.ShapeDtypeStruct((n // tm, SUBLANES, CONV_DIM), F32),)
        out_specs += (pl.BlockSpec((1, SUBLANES, CONV_DIM), lambda i: (i, 0, 0)),)
        scratch = [pltpu.VMEM((tm + SUBLANES, CONV_DIM), F32)]
    return pl.pallas_call(
        functools.partial(_inproj_kernel, page, seq_len // tm if fused_conv else None),
        grid=(n // tm,),
        in_specs=in_specs,
        out_specs=out_specs,
        out_shape=out_shapes,
        scratch_shapes=scratch,
        compiler_params=_params("arbitrary" if fused_conv else "parallel"),
        name="inproj",
    )(*operands)


def _sb_tile(z, carry, ntri, mask):
    sp = _softplus(z)
    if mask is not None:
        sp = jnp.where(mask, sp, 0.0)
    suffix = _dot(sp.astype(BF16), ntri)
    w = jnp.exp(z - sp + suffix + carry)
    if mask is not None:
        w = jnp.where(mask, w, 0.0)
    return w.astype(BF16), carry - jnp.sum(sp, axis=1, keepdims=True)


def _sb_prompt_kernel(bias_ref, q_ref, k_ref, v_ref, ntri_ref, o_ref,
                      qs_scr, bias_scr, z_scr, beta_scr, decay_scr, run_scr, acc_scr):
    hp = pl.program_id(1)
    tq, tk = SB_QBLOCK, SB_BLOCK
    ratio = tq // tk
    n_q = q_ref.shape[1] // tq
    n_tiles = ratio * n_q * (n_q + 1) // 2
    lane = lax.broadcasted_iota(jnp.int32, (1, LANES), 1)
    head_lanes = [lane < SB_HEAD_DIM, lane >= SB_HEAD_DIM]
    zero = jnp.zeros((), BF16)
    rows = [slice(j * tq, (j + 1) * tq) for j in range(2)]

    for ref in (z_scr, beta_scr, decay_scr, run_scr, acc_scr):
        ref[...] = jnp.zeros_like(ref)
    row = lax.broadcasted_iota(jnp.int32, (tq, tk), 0)
    col = lax.broadcasted_iota(jnp.int32, (tq, tk), 1)
    for j in range(2):
        bias = bias_ref[2 * hp + j]
        bias_scr[0, j] = jnp.full((tq, tk), bias, F32)
        for m in range(ratio):
            bias_scr[1 + m, j] = jnp.where(m * tk + col < row, bias, -1e30)
    for i in range(n_q):
        q = q_ref[0, i * tq:(i + 1) * tq, :]
        for j in range(2):
            qs_scr[i, rows[j], :] = jnp.where(head_lanes[j], q, zero)

    def key_block(ref, kb):
        return ref[0, pl.ds(pl.multiple_of(kb * tk, tk), tk), :]

    def step(s, tiles):
        (q_a, k_a), (q_b, k_b), (q_d, k_d) = tiles
        chunks = [slice(c * SB_ROW_CHUNK, (c + 1) * SB_ROW_CHUNK) for c in range(2 * tq // SB_ROW_CHUNK)]
        zs = [z_scr[r, :] for r in chunks]
        z_scr[...] = _dot_nt(qs_scr[q_a], key_block(k_ref, k_a))
        valid_d = jnp.logical_and(s >= 2, s - 2 < n_tiles)
        v = key_block(v_ref, k_d)
        w = jnp.concatenate([beta_scr[rows[j], :] * decay_scr[rows[j], :] for j in range(2)], axis=1)
        vm = jnp.concatenate(
            [jnp.where(jnp.logical_and(head_lanes[j], valid_d), v, zero) for j in range(2)], axis=0)
        pv = _dot(w, vm)
        acc_rows = pl.ds(pl.multiple_of(q_d * tq, tq), tq)
        acc_scr[acc_rows, :] = acc_scr[acc_rows, :] + pv
        first_of_block = k_b == ratio * q_b + ratio - 1
        offset = jnp.maximum(k_b - ratio * q_b + 1, 0)
        one = jnp.ones((), BF16)
        sps = []
        for r, z_raw in zip(chunks, zs):
            j, r_in_head = divmod(r.start, tq)
            z = z_raw + bias_scr[offset, j, r_in_head:r_in_head + SB_ROW_CHUNK, :]
            t = jnp.exp2(jnp.abs(z) * NEG_LOG2E)
            log1p_t = jnp.log(1.0 + t).astype(BF16)
            z16, t16 = z.astype(BF16), t.astype(BF16)
            sps.append(jnp.maximum(z16, zero) + log1p_t)
            beta_scr[r, :] = jnp.where(z16 >= zero, one, t16) / (one + t16)
        suffix = _dot(jnp.concatenate(sps, axis=0), ntri_ref[...])
        for r, sp in zip(chunks, sps):
            run = jnp.where(first_of_block, 0.0, run_scr[r, :])
            decay_scr[r, :] = jnp.exp(suffix[r, :] + run).astype(BF16)
            run_scr[r, :] = run + suffix[r, 0:1] - sp[:, 0:1].astype(F32)
        wrap = k_a == 0
        q_n = jnp.where(wrap, jnp.minimum(q_a + 1, n_q - 1), q_a)
        k_n = jnp.where(wrap, ratio * q_n + ratio - 1, k_a - 1)
        return ((q_n, k_n),) + tiles[:2]

    first = (jnp.int32(0), jnp.int32(ratio - 1))
    lax.fori_loop(0, n_tiles + 2, step, (first,) * 3)
    o_ref[0] = acc_scr[...].astype(BF16)


def _sb_prompt(sb_bias, q, k, v, ntri):
    bsz, t_len, _ = q.shape
    tq, tk = SB_QBLOCK, SB_BLOCK
    seq = pl.BlockSpec((1, t_len, LANES), lambda b, h, *_: (b, 0, h))
    return pl.pallas_call(
        _sb_prompt_kernel,
        grid_spec=pltpu.PrefetchScalarGridSpec(
            num_scalar_prefetch=1,
            grid=(bsz, SB_DIM // LANES),
            in_specs=[seq, seq, seq, pl.BlockSpec((tk, tk), lambda b, h, *_: (0, 0))],
            out_specs=seq,
            scratch_shapes=[
                pltpu.VMEM((t_len // tq, 2 * tq, LANES), BF16),
                pltpu.VMEM((1 + tq // tk, 2, tq, tk), F32),
                pltpu.VMEM((2 * tq, tk), F32),
                pltpu.VMEM((2 * tq, tk), BF16),
                pltpu.VMEM((2 * tq, tk), BF16),
                pltpu.VMEM((2 * tq, 1), F32),
                pltpu.VMEM((t_len, LANES), F32),
            ],
        ),
        out_shape=jax.ShapeDtypeStruct((bsz, t_len, SB_DIM), BF16),
        compiler_params=_params("parallel", "parallel"),
        name="sb_prompt",
    )(sb_bias, q, k, v, ntri)


def _sb_decode_kernel(n_pages, t_new, pt_ref, bias_ref, q_ref, kn_ref, vn_ref, ntri_ref, *refs):
    kt_pages = refs[:n_pages]
    vt_pages = refs[n_pages:2 * n_pages]
    o_ref = refs[2 * n_pages]
    page = kt_pages[0].shape[2]
    rows = t_new * SB_HEADS
    r_iota = lax.broadcasted_iota(jnp.int32, (rows, SB_DIM), 0)
    l_iota = lax.broadcasted_iota(jnp.int32, (rows, SB_DIM), 1)
    head_of_row = r_iota % SB_HEADS
    own_lanes = (l_iota // SB_HEAD_DIM) == head_of_row

    q = q_ref[0].astype(F32)
    q_rep = jnp.broadcast_to(q[:, None, :], (t_new, SB_HEADS, SB_DIM)).reshape(rows, SB_DIM)
    q_bd = jnp.where(own_lanes, q_rep, 0.0).astype(BF16)

    r1 = lax.broadcasted_iota(jnp.int32, (rows, 1), 0)
    bias = jnp.zeros((rows, 1), F32)
    for h in range(SB_HEADS):
        bias = jnp.where(r1 % SB_HEADS == h, bias_ref[h], bias)

    pad = jnp.zeros((page - SUBLANES, SB_DIM), F32)
    k_new = jnp.concatenate([kn_ref[0], pad], axis=0).astype(BF16)
    v_new = jnp.concatenate([vn_ref[0], pad], axis=0).astype(BF16)
    kcol = lax.broadcasted_iota(jnp.int32, (rows, page), 1)
    qtok = lax.broadcasted_iota(jnp.int32, (rows, page), 0) // SB_HEADS
    z = _dot_nt(q_bd, k_new) + bias
    w, carry = _sb_tile(z, jnp.zeros((rows, 1), F32), ntri_ref[:page, :page], kcol < qtok)
    acc = _dot(w, v_new)

    blk = ntri_ref.shape[0] // page
    order = range(n_pages // blk - 1, -1, -1)
    lane_cat = lambda pages, pb: jnp.concatenate(
        [pages[pb * blk + i][0] for i in range(blk)], axis=1).astype(BF16)
    zs = [_dot(q_bd, lane_cat(kt_pages, pb)) + bias for pb in order]
    sps = [_softplus(z) for z in zs]
    suffixes = [_dot(sp.astype(BF16), ntri_ref[...]) for sp in sps]
    for pb, z, sp, suffix in zip(order, zs, sps, suffixes):
        w = jnp.exp(z - sp + suffix + carry).astype(BF16)
        carry = carry - jnp.sum(sp, axis=1, keepdims=True)
        acc = acc + _dot_nt(w, lane_cat(vt_pages, pb))

    acc = jnp.where(own_lanes, acc, 0.0)
    o_ref[0] = jnp.sum(acc.reshape(t_new, SB_HEADS, SB_DIM), axis=1).astype(BF16)


def _sb_decode(page_table, sb_bias, q, k_new8, v_new8, ntri, cache_kt, cache_vt):
    n_seq, t_new, _ = q.shape
    n_pages = page_table.shape[1]
    page = cache_kt.shape[2]

    def page_spec(p):
        return pl.BlockSpec((1, SB_DIM, page), lambda s, pt, b: (pt[s, p], 0, 0))

    seq_spec = lambda r: pl.BlockSpec((1, r, SB_DIM), lambda s, pt, b: (s, 0, 0))
    return pl.pallas_call(
        functools.partial(_sb_decode_kernel, n_pages, t_new),
        grid_spec=pltpu.PrefetchScalarGridSpec(
            num_scalar_prefetch=2,
            grid=(n_seq,),
            in_specs=[seq_spec(t_new), seq_spec(SUBLANES), seq_spec(SUBLANES),
                      pl.BlockSpec(ntri.shape, lambda s, pt, b: (0, 0))]
                     + [page_spec(p) for p in range(n_pages)] * 2,
            out_specs=seq_spec(t_new),
        ),
        out_shape=jax.ShapeDtypeStruct((n_seq, t_new, SB_DIM), BF16),
        compiler_params=_params("parallel"),
        name="sb_decode",
    )(page_table, sb_bias, q, k_new8, v_new8, ntri, *([cache_kt] * n_pages), *([cache_vt] * n_pages))


def _conv_silu(ext_ref, first, rows, cw_ref, cols=slice(None)):
    acc = None
    for w in range(CONV_WIDTH):
        term = ext_ref[first + w:first + w + rows, cols] * cw_ref[w:w + 1, cols]
        acc = term if acc is None else acc + term
    return _silu(acc)


def _l2norm(x):
    return x * lax.rsqrt(jnp.sum(x * x, axis=-1, keepdims=True) + L2_EPS)


def _gate_terms(ab, alog_ref, dtb_ref):
    g = -jnp.exp(alog_ref[...]) * _softplus(ab + dtb_ref[...])
    return g, _sigmoid(ab)


def _gated_out_norm(o, z, gw):
    o = o * lax.rsqrt(jnp.mean(o * o, axis=-1, keepdims=True) + NORM_EPS) * gw
    return o * _silu(z)


def _gdn_prep_kernel(conv_ref, ab_ref, alog_ref, dtb_ref, lincl_ref, hones_ref,
                     un_ref, wn_ref, qg_ref, kg_ref, att_ref, cg_ref):
    rows = conv_ref.shape[1]
    c_len = GDN_CHUNK
    dk = GDN_HEAD_DIM
    conv = conv_ref[0]

    g_all, beta_all = _gate_terms(ab_ref[0], alog_ref, dtb_ref)
    r_i = lax.broadcasted_iota(jnp.int32, (c_len, c_len), 0)
    c_i = lax.broadcasted_iota(jnp.int32, (c_len, c_len), 1)
    incl = c_i <= r_i
    strict = c_i < r_i
    lincl = lincl_ref[...]
    n_chunks = rows // c_len

    cg_chunks = [_dot_exact01(lincl, g_all[c * c_len:(c + 1) * c_len, :]) for c in range(n_chunks)]
    cg_t_pairs = []
    for pair in range(n_chunks // 2):
        cg_pair = jnp.concatenate(cg_chunks[2 * pair:2 * pair + 2], axis=0)
        cg_ref[0, 2 * pair * c_len:(2 * pair + 2) * c_len, :] = cg_pair
        cg_t_pairs.append(cg_pair.T)

    problems = [(c, h) for c in range(n_chunks) for h in range(GDN_HEADS)]

    def head_cols(base, h):
        return slice(base + h * dk, base + (h + 1) * dk)

    def l2norm_heads(x):
        sq = x * x
        hi = sq.astype(BF16)
        lo = (sq - hi.astype(F32)).astype(BF16)
        return x * lax.rsqrt(_dot(hi, hones_ref[...]) + _dot(lo, hones_ref[...]) + L2_EPS)

    q_all = l2norm_heads(conv[:, :GDN_DIM]) * (dk ** -0.5)
    k_all = l2norm_heads(conv[:, GDN_DIM:2 * GDN_DIM])

    qs, ks, vs, bcs, cgcols, decays = [], [], [], [], [], []
    for c, h in problems:
        r = slice(c * c_len, (c + 1) * c_len)
        qs.append(q_all[r, head_cols(0, h)])
        ks.append(k_all[r, head_cols(0, h)])
        vs.append(conv[r, head_cols(2 * GDN_DIM, h)])
        bcs.append(beta_all[r, GDN_HEADS + h:GDN_HEADS + h + 1])
        cg_col = cg_chunks[c][:, h:h + 1]
        cg_row = cg_t_pairs[c // 2][h:h + 1, (c % 2) * c_len:(c % 2 + 1) * c_len]
        cgcols.append(cg_col)
        decays.append(jnp.exp(jnp.where(incl, cg_col - cg_row, -jnp.inf)))
    kbs = [k.astype(BF16) for k in ks]
    kks = [_dot_nt(kb, kb) for kb in kbs]

    powers = [jnp.where(strict, -(bc * kk * dec), 0.0) for bc, kk, dec in zip(bcs, kks, decays)]
    inv_off = powers
    for _ in range(int(math.log2(c_len)) - 1):
        pbs = [p.astype(BF16) for p in powers]
        powers = [_dot(pb, pb) for pb in pbs]
        next_pbs = [p.astype(BF16) for p in powers]
        inv_off = [t + p + _dot(t.astype(BF16), pb) for t, p, pb in zip(inv_off, powers, next_pbs)]

    for (c, h), q, k, v, bc, cg_col, dec, t_off in zip(problems, qs, ks, vs, bcs, cgcols, decays, inv_off):
        r = slice(c * c_len, (c + 1) * c_len)
        cols = head_cols(0, h)
        rhs = jnp.concatenate([v * bc, k * (bc * jnp.exp(cg_col))], axis=1)
        sol = rhs + _dot(t_off.astype(BF16), rhs.astype(BF16))
        un_ref[0, r, cols] = sol[:, :dk]
        wn_ref[0, r, cols] = sol[:, dk:].astype(BF16)
        qg_ref[0, r, cols] = (q * jnp.exp(cg_col)).astype(BF16)
        g_last = cg_chunks[c][c_len - 1:c_len, h:h + 1]
        kg_ref[0, r, cols] = (k * jnp.exp(g_last - cg_col)).astype(BF16)
        qk = _dot_nt(q.astype(BF16), k.astype(BF16))
        att_ref[0, h, r, :] = (qk * dec).astype(BF16)


def _gdn_prep(conv, ab, alog_pad, dtb_pad, lincl):
    bsz, t_len, _ = conv.shape
    head_of = jnp.arange(GDN_DIM) // GDN_HEAD_DIM
    head_ones = (head_of[:, None] == head_of[None, :]).astype(BF16)
    rows = min(PREP_ROWS, t_len)
    blk = lambda width: pl.BlockSpec((1, rows, width), lambda b, i: (b, i, 0))
    out_shapes = (
        jax.ShapeDtypeStruct((bsz, t_len, GDN_DIM), F32),
        jax.ShapeDtypeStruct((bsz, t_len, GDN_DIM), BF16),
        jax.ShapeDtypeStruct((bsz, t_len, GDN_DIM), BF16),
        jax.ShapeDtypeStruct((bsz, t_len, GDN_DIM), BF16),
        jax.ShapeDtypeStruct((bsz, GDN_HEADS, t_len, GDN_CHUNK), BF16),
        jax.ShapeDtypeStruct((bsz, t_len, LANES), F32),
    )
    return pl.pallas_call(
        _gdn_prep_kernel,
        grid=(bsz, t_len // rows),
        in_specs=[
            blk(CONV_DIM),
            blk(LANES),
            _const_spec((1, LANES)),
            _const_spec((1, LANES)),
            _const_spec((GDN_CHUNK, GDN_CHUNK)),
            _const_spec((GDN_DIM, GDN_DIM)),
        ],
        out_specs=(blk(GDN_DIM), blk(GDN_DIM), blk(GDN_DIM), blk(GDN_DIM),
                   pl.BlockSpec((1, GDN_HEADS, rows, GDN_CHUNK), lambda b, i: (b, 0, i, 0)),
                   blk(LANES)),
        out_shape=out_shapes,
        compiler_params=_params("parallel", "parallel"),
        name="gdn_prep",
    )(conv, ab, alog_pad, dtb_pad, lincl, head_ones)


def _gdn_scan_kernel(un_ref, wn_ref, qg_ref, kg_ref, att_ref, cg_ref, z_ref, gw_ref, o_ref, s_out_ref, s_ref):
    c = pl.program_id(0)
    bsz = un_ref.shape[0]
    c_len = GDN_CHUNK
    dk = GDN_HEAD_DIM

    @pl.when(c == 0)
    def _():
        s_ref[...] = jnp.zeros_like(s_ref)

    chains = [(b, h, slice(h * dk, (h + 1) * dk)) for b in range(bsz) for h in range(GDN_HEADS)]
    states = [s_ref[b, h] for b, h, _ in chains]
    for sub in range(un_ref.shape[1] // c_len):
        r = slice(sub * c_len, (sub + 1) * c_len)
        decay_last = [jnp.exp(cg_ref[b, r.stop - 1:r.stop, :]) for b in range(bsz)]
        sbs = [s.astype(BF16) for s in states]
        ws = [_dot(wn_ref[b, r, cols], sb) for (b, _, cols), sb in zip(chains, sbs)]
        qs = [_dot(qg_ref[b, r, cols], sb) for (b, _, cols), sb in zip(chains, sbs)]
        vbs = [(un_ref[b, r, cols] - w).astype(BF16) for (b, _, cols), w in zip(chains, ws)]
        new_states = []
        for (b, h, cols), s, q_s, vb in zip(chains, states, qs, vbs):
            o = q_s + _dot(att_ref[b, h, r, :], vb)
            new_states.append(s * decay_last[b][:, h:h + 1] + _dot_tn(kg_ref[b, r, cols], vb))
            o_ref[b, r, cols] = _gated_out_norm(o, z_ref[b, r, cols].astype(F32), gw_ref[...]).astype(BF16)
        states = new_states
    for (b, h, _), s in zip(chains, states):
        s_ref[b, h] = s

    @pl.when(c == pl.num_programs(0) - 1)
    def _():
        s_out_ref[...] = s_ref[...]


def _gdn_scan(un, wn, qg, kg, att, cg, z, gw):
    bsz, t_len, _ = un.shape
    rows = min(SCAN_ROWS, t_len)
    blk = lambda width: pl.BlockSpec((bsz, rows, width), lambda c: (0, c, 0))
    state_shape = (bsz, GDN_HEADS, GDN_HEAD_DIM, GDN_HEAD_DIM)
    return pl.pallas_call(
        _gdn_scan_kernel,
        grid=(t_len // rows,),
        in_specs=[blk(GDN_DIM), blk(GDN_DIM), blk(GDN_DIM), blk(GDN_DIM),
                  pl.BlockSpec((bsz, GDN_HEADS, rows, GDN_CHUNK), lambda c: (0, 0, c, 0)),
                  blk(LANES), blk(GDN_DIM), _const_spec((1, GDN_HEAD_DIM))],
        out_specs=(blk(GDN_DIM), pl.BlockSpec(state_shape, lambda c: (0, 0, 0, 0))),
        out_shape=(jax.ShapeDtypeStruct((bsz, t_len, GDN_DIM), BF16),
                   jax.ShapeDtypeStruct(state_shape, F32)),
        scratch_shapes=[pltpu.VMEM(state_shape, F32)],
        compiler_params=_params("arbitrary"),
        name="gdn_scan",
    )(un, wn, qg, kg, att, cg, z, gw)


def _gdn_decode_kernel(t_new, u_ref, hist_ref, ab_ref, z_ref, s_in_ref, cw_ref, alog_ref, dtb_ref, gw_ref,
                       o_ref, s_out_ref, ext_ref, ab_scr):
    dk = GDN_HEAD_DIM
    n_hist = CONV_WIDTH - 1
    n_seq = u_ref.shape[0]
    tile = (SUBLANES, LANES)
    tail = jnp.zeros((LANES - SUBLANES, LANES), F32)
    pad_rows = lambda x: jnp.concatenate([x, tail], axis=0)
    row = lax.broadcasted_iota(jnp.int32, tile, 0)
    lane = lax.broadcasted_iota(jnp.int32, tile, 1)
    real = row < t_new
    ext_ref[...] = jnp.zeros_like(ext_ref)
    ab_scr[...] = jnp.zeros_like(ab_scr)

    problems = [(g, h) for g in range(n_seq) for h in range(GDN_HEADS)]
    q8, k8, v8, e_col, b_col, w_col, decay, e_last = {}, {}, {}, {}, {}, {}, {}, {}
    for g in range(n_seq):
        ext_ref[g, 0:n_hist, :] = hist_ref[g]
        ext_ref[g, n_hist:n_hist + t_new, :] = u_ref[g]
        ab_scr[g, 0:t_new, :] = ab_ref[g]
        conv = _conv_silu(ext_ref.at[g], 0, SUBLANES, cw_ref)
        g_all, beta_all = _gate_terms(ab_scr[g], alog_ref, dtb_ref)
        cg = g_all
        for shift in range(1, t_new):
            cg = cg + jnp.where(row >= shift, pltpu.roll(g_all, shift, axis=0), 0.0)
        cg_last = cg[t_new - 1:t_new, :]
        e_all = jnp.exp(cg)
        w_all = jnp.exp(cg_last - cg)
        cg_t = pad_rows(cg).T
        for h in range(GDN_HEADS):
            p = (g, h)
            q8[p] = _l2norm(conv[:, h * dk:(h + 1) * dk]) * (dk ** -0.5)
            k8[p] = _l2norm(conv[:, GDN_DIM + h * dk:GDN_DIM + (h + 1) * dk])
            v8[p] = conv[:, 2 * GDN_DIM + h * dk:2 * GDN_DIM + (h + 1) * dk]
            e_col[p] = e_all[:, h:h + 1]
            b_col[p] = beta_all[:, GDN_HEADS + h:GDN_HEADS + h + 1]
            w_col[p] = w_all[:, h:h + 1]
            e_last[p] = e_all[t_new - 1:t_new, h:h + 1]
            decay[p] = jnp.exp(jnp.where(lane <= row, cg[:, h:h + 1] - cg_t[h:h + 1, :], -jnp.inf))

    states = {p: s_in_ref[p[0], p[1]] for p in problems}
    kq = {p: jnp.concatenate([k8[p], q8[p]], axis=0).astype(BF16) for p in problems}
    ks0_qs0 = {p: _dot(kq[p], states[p].astype(BF16)) for p in problems}
    gram = {p: _dot_nt(kq[p], pad_rows(k8[p]).astype(BF16)) for p in problems}
    deltas = {}
    for p in problems:
        m = b_col[p] * jnp.where(lane < row, gram[p][:SUBLANES] * decay[p], 0.0)
        r = b_col[p] * (v8[p] - e_col[p] * ks0_qs0[p][:SUBLANES])
        for j in range(t_new - 1):
            r = r - m[:, j:j + 1] * r[j:j + 1, :]
        deltas[p] = jnp.where(real, r, 0.0)
    updates = {}
    for p in problems:
        kw_t = pad_rows(jnp.where(real, k8[p] * w_col[p], 0.0)).T
        updates[p] = _dot(kw_t.astype(BF16), pad_rows(deltas[p]).astype(BF16))
    for g, h in problems:
        p = (g, h)
        cols = slice(h * dk, (h + 1) * dk)
        coef = gram[p][SUBLANES:] * decay[p]
        o = e_col[p] * ks0_qs0[p][SUBLANES:]
        for j in range(t_new):
            o = o + coef[:, j:j + 1] * deltas[p][j:j + 1, :]
        o = _gated_out_norm(o[:t_new], z_ref[g, :, cols].astype(F32), gw_ref[...])
        o_ref[g, :, cols] = o.astype(BF16)
        s_out_ref[g, h] = states[p] * e_last[p] + updates[p]


def _gdn_decode(u, hist, ab, z, s_in, conv_w, alog_pad, dtb_pad, gw):
    n_seq, t_new, _ = u.shape
    grp = DEC_GROUP
    seq = lambda r, width: pl.BlockSpec((grp, r, width), lambda i: (i, 0, 0))
    state = pl.BlockSpec((grp, GDN_HEADS, GDN_HEAD_DIM, GDN_HEAD_DIM), lambda i: (i, 0, 0, 0))
    return pl.pallas_call(
        functools.partial(_gdn_decode_kernel, t_new),
        grid=(n_seq // grp,),
        in_specs=[seq(t_new, CONV_DIM), seq(CONV_WIDTH - 1, CONV_DIM), seq(t_new, LANES), seq(t_new, GDN_DIM),
                  state, _const_spec((CONV_WIDTH, CONV_DIM)), _const_spec((1, LANES)), _const_spec((1, LANES)),
                  _const_spec((1, GDN_HEAD_DIM))],
        out_specs=(seq(t_new, GDN_DIM), state),
        out_shape=(jax.ShapeDtypeStruct((n_seq, t_new, GDN_DIM), BF16),
                   jax.ShapeDtypeStruct(s_in.shape, F32)),
        scratch_shapes=[pltpu.VMEM((grp, 2 * SUBLANES, CONV_DIM), F32), pltpu.VMEM((grp, SUBLANES, LANES), F32)],
        compiler_params=_params("parallel"),
        name="gdn_decode",
    )(u, hist, ab, z, s_in, conv_w, alog_pad, dtb_pad, gw)


def _tail_kernel(x_ref, oa_ref, ob_ref, sg_ref, wpa_ref, wpb_ref, wo_ref, nw_ref, wup_ref, wdown_ref, nf_ref,
                 y_ref, acc_ref, hm_ref):
    f = pl.program_id(1)

    @pl.when(f == 0)
    def _():
        for r0 in range(0, x_ref.shape[0], MERGE_ROWS):
            r = slice(r0, min(r0 + MERGE_ROWS, x_ref.shape[0]))
            y_a = _dot(oa_ref[r, :], wpa_ref[...])
            y_b = _dot(ob_ref[r, :], wpb_ref[...])
            mix = sg_ref[r, :D_MODEL] * y_a + sg_ref[r, D_MODEL:] * y_b
            x1 = x_ref[r, :] + _dot(mix.astype(BF16), wo_ref[...])
            acc_ref[r, :] = x1
            hm_ref[r, :] = _rmsnorm(x1, nw_ref[...]).astype(BF16)

    up = jnp.maximum(_dot(hm_ref[...], wup_ref[...]), 0.0)
    acc_ref[...] += _dot((up * up).astype(BF16), wdown_ref[...])

    @pl.when(f == pl.num_programs(1) - 1)
    def _():
        y_ref[...] = _rmsnorm(acc_ref[...], nf_ref[...])


def _tail(x2d, o_a, o_b, sg, lw, norm_f, tm, tf):
    n = x2d.shape[0]
    row = lambda width: pl.BlockSpec((tm, width), lambda i, f: (i, 0))
    return pl.pallas_call(
        _tail_kernel,
        grid=(n // tm, D_FF // tf),
        in_specs=[row(D_MODEL), row(SB_DIM), row(GDN_DIM), row(2 * D_MODEL),
                  _const_spec(lw["w_pa"].shape), _const_spec(lw["w_pb"].shape), _const_spec(lw["w_o"].shape),
                  _const_spec((1, D_MODEL)),
                  pl.BlockSpec((D_MODEL, tf), lambda i, f: (0, f)),
                  pl.BlockSpec((tf, D_MODEL), lambda i, f: (f, 0)),
                  _const_spec((1, D_MODEL))],
        out_specs=row(D_MODEL),
        out_shape=jax.ShapeDtypeStruct((n, D_MODEL), F32),
        scratch_shapes=[pltpu.VMEM((tm, D_MODEL), F32), pltpu.VMEM((tm, D_MODEL), BF16)],
        compiler_params=_params("parallel", "arbitrary"),
        name="tail",
    )(x2d, o_a, o_b, sg, lw["w_pa"], lw["w_pb"], lw["w_o"], lw["norm_mlp_w"], lw["w_up"], lw["w_down"], norm_f)


def _permute_w_in(w):
    n_main = 3 * SB_DIM + CONV_DIM + GDN_DIM
    gates = w[:, n_main + 2 * GDN_HEADS:]
    ab = w[:, n_main:n_main + 2 * GDN_HEADS]
    pad = jnp.zeros((w.shape[0], LANES - 2 * GDN_HEADS), w.dtype)
    return jnp.concatenate([w[:, :n_main], gates, ab, pad], axis=1).astype(BF16)


def _lane_pad(vec, offset):
    return jnp.zeros((1, LANES), F32).at[0, offset:offset + vec.shape[0]].set(vec.astype(F32))


def _layer_common(x2d, lw, tm, **kwargs):
    return _inproj(x2d, lw["norm_mix_w"], lw["w_in"], tm, **kwargs)


def kernel(x_prompt, x_sample, cache_k, cache_v, page_table, state_conv, state_ssm, norm_mix_w, w_in, sb_bias,
           conv_w, a_log, dt_bias, gdn_norm_w, w_pa, w_pb, w_o, norm_mlp_w, w_up, w_down, norm_final_w):
    depth = w_in.shape[0]
    assert depth == 1, "the residual stream is normalised once, after the only layer"
    b_p, t_p, _ = x_prompt.shape
    b_s, t_s, _ = x_sample.shape
    n_pool, page = cache_k.shape[1], cache_k.shape[2]
    l = 0
    lw = {
        "norm_mix_w": norm_mix_w[l].reshape(1, D_MODEL).astype(F32),
        "w_in": _permute_w_in(w_in[l]),
        "w_pa": w_pa[l].astype(BF16), "w_pb": w_pb[l].astype(BF16), "w_o": w_o[l].astype(BF16),
        "norm_mlp_w": norm_mlp_w[l].reshape(1, D_MODEL).astype(F32),
        "w_up": w_up[l].astype(BF16), "w_down": w_down[l].astype(BF16),
    }
    norm_f = norm_final_w.reshape(1, D_MODEL).astype(F32)
    bias = sb_bias[l].astype(F32)
    cw = conv_w[l].astype(F32)
    alog_pad = _lane_pad(a_log[l], 0)
    dtb_pad = _lane_pad(dt_bias[l], 0)
    gw = gdn_norm_w[l].reshape(1, GDN_HEAD_DIM).astype(F32)
    ntri = -jnp.tril(jnp.ones((SB_BLOCK, SB_BLOCK), BF16), -1)
    lincl = jnp.tril(jnp.ones((GDN_CHUNK, GDN_CHUNK), BF16))

    xp = x_prompt.reshape(b_p * t_p, D_MODEL).astype(F32)
    tm_p = 512
    q, k_t, v_t, kb, vb, conv, z, ab, sg, u_tail = _layer_common(xp, lw, tm_p, page=page, conv_w=cw, seq_len=t_p)
    shp = lambda a: a.reshape(b_p, t_p, a.shape[-1])
    o_a = _sb_prompt(bias, shp(q), shp(kb), shp(vb), ntri)
    un, wn, qg, kg, att, cg = _gdn_prep(shp(conv), shp(ab), alog_pad, dtb_pad, lincl)
    o_b, ssm_p = _gdn_scan(un, wn, qg, kg, att, cg, shp(z), gw)
    y_p = _tail(xp, o_a.reshape(-1, SB_DIM), o_b.reshape(-1, GDN_DIM), sg, lw, norm_f, 1024, 1024)
    y_prompt = y_p.reshape(b_p, t_p, D_MODEL).astype(x_prompt.dtype)
    as_pages = lambda a_t: jnp.transpose(
        a_t.reshape(b_p, t_p // page, SB_HEADS, SB_HEAD_DIM, page), (0, 1, 4, 2, 3))[None]
    new_k_prompt = as_pages(k_t).astype(cache_k.dtype)
    new_v_prompt = as_pages(v_t).astype(cache_v.dtype)
    seq_tails = u_tail.reshape(b_p, t_p // tm_p, SUBLANES, CONV_DIM)[:, -1]
    new_conv_prompt = seq_tails[None, :, SUBLANES - (CONV_WIDTH - 1):, :].astype(state_conv.dtype)
    new_ssm_prompt = ssm_p[None].astype(state_ssm.dtype)

    xs = x_sample.reshape(b_s * t_s, D_MODEL).astype(F32)
    tm_s = min(256, b_s * t_s)
    q, k, v, _, _, u, z, ab, sg = _layer_common(xs, lw, tm_s)
    shs = lambda a: a.reshape(b_s, t_s, a.shape[-1])
    pad8 = lambda a: jnp.pad(shs(a), ((0, 0), (0, SUBLANES - t_s), (0, 0)))
    pages_t = lambda c: jnp.transpose(c[l], (0, 2, 3, 1)).reshape(n_pool, SB_DIM, page).astype(F32)
    o_a = _sb_decode(page_table, bias, shs(q), pad8(k), pad8(v), ntri, pages_t(cache_k), pages_t(cache_v))
    o_b, ssm_s = _gdn_decode(shs(u), state_conv[l].astype(F32), shs(ab), shs(z), state_ssm[l].astype(F32),
                             cw, alog_pad, dtb_pad, gw)
    y_s = _tail(xs, o_a.reshape(-1, SB_DIM), o_b.reshape(-1, GDN_DIM), sg, lw, norm_f, min(512, b_s * t_s), 1024)
    y_sample = y_s.reshape(b_s, t_s, D_MODEL).astype(x_sample.dtype)
    head_shape = (depth, b_s, t_s, SB_HEADS, SB_HEAD_DIM)
    new_k_sample = k.reshape(head_shape).astype(cache_k.dtype)
    new_v_sample = v.reshape(head_shape).astype(cache_v.dtype)
    new_conv_sample = jnp.concatenate([state_conv[l].astype(F32), shs(u)], axis=1)[None, :, t_s:, :].astype(state_conv.dtype)
    new_ssm_sample = ssm_s[None].astype(state_ssm.dtype)

    return (y_prompt, y_sample, new_k_prompt, new_v_prompt, new_k_sample, new_v_sample,
            new_conv_prompt, new_conv_sample, new_ssm_prompt, new_ssm_sample)
```
